```python
import jax, jax.numpy as jnp
from jax import lax
import numpy as np

D_MODEL = 1024
BATCH = 8
SEQ = 4096
DEPTH = 1
DEC_BATCH = 8
DEC_SEQ = 2048
PAST_LEN = 128

EPS = 1e-6
MLA_HEADS = 8
Q_LORA = 256
KV_LORA = 128
QK_NOPE = 64
ROPE_DIM = 32
QK_HEAD = QK_NOPE + ROPE_DIM
V_HEAD = 64
MLA_WIDTH = MLA_HEADS * V_HEAD
ROPE_THETA = 10000.0
Q_BLOCK = 128
NA_HEADS = 8
NA_HEAD_DIM = 64
NA_WIDTH = NA_HEADS * NA_HEAD_DIM
GRID_W = 64
NA_KH_MAX = 8
NA_KW = 16
MIX_WIDTH = MLA_WIDTH + NA_WIDTH
IN_SPLITS = (Q_LORA, KV_LORA, ROPE_DIM, NA_WIDTH, NA_WIDTH, NA_WIDTH)
IN_DIM = sum(IN_SPLITS)
N_EXPERTS = 32
TOP_K = 4
D_FF = 1024
SWIGLU_LIMIT = 7.0
SWIGLU_ALPHA = 1.702
MOE_BLOCK = 128

kernel_name = "hymba_mla_natten_moe_encoder"


def rms_norm(x, g):
    xf = x.astype(jnp.float32)
    y = xf * lax.rsqrt(jnp.mean(xf * xf, axis=-1, keepdims=True) + EPS)
    return (y * g.astype(jnp.float32)).astype(x.dtype)


def rope_tables(T):
    pos = jnp.arange(T, dtype=jnp.float32)
    inv = 1.0 / (ROPE_THETA ** (jnp.arange(0, ROPE_DIM, 2, dtype=jnp.float32) / ROPE_DIM))
    ang = pos[:, None] * inv[None, :]
    return jnp.cos(ang), jnp.sin(ang)


def apply_rope(x, cos, sin):
    half = x.shape[-1] // 2
    x1 = x[..., :half].astype(jnp.float32)
    x2 = x[..., half:].astype(jnp.float32)
    out = jnp.concatenate([x1 * cos - x2 * sin, x1 * sin + x2 * cos], axis=-1)
    return out.astype(x.dtype)


def mla_attention(c_q, c_kv, k_pe, g_q, w_uq, g_kv, w_ukv):
    B, T, _ = c_q.shape
    q = (rms_norm(c_q, g_q) @ w_uq).reshape(B, T, MLA_HEADS, QK_HEAD)
    q_nope, q_pe = q[..., :QK_NOPE], q[..., QK_NOPE:]
    kv = (rms_norm(c_kv, g_kv) @ w_ukv).reshape(B, T, MLA_HEADS, QK_NOPE + V_HEAD)
    k_nope, v = kv[..., :QK_NOPE], kv[..., QK_NOPE:]
    cos, sin = rope_tables(T)
    q_pe = apply_rope(q_pe, cos[None, :, None, :], sin[None, :, None, :])
    k_pe = apply_rope(k_pe, cos[None], sin[None])
    q = jnp.concatenate([q_nope, q_pe], axis=-1)
    k = jnp.concatenate([k_nope, jnp.broadcast_to(k_pe[:, :, None, :], (B, T, MLA_HEADS, ROPE_DIM))], axis=-1)
    scale = QK_HEAD ** -0.5
    nb = T // Q_BLOCK
    qb = q.reshape(B, nb, Q_BLOCK, MLA_HEADS, QK_HEAD).transpose(1, 0, 2, 3, 4)

    def block(qi):
        s = jnp.einsum('bqhd,bkhd->bhqk', qi, k).astype(jnp.float32) * scale
        p = jax.nn.softmax(s, axis=-1).astype(v.dtype)
        return jnp.einsum('bhqk,bkhd->bqhd', p, v)

    o = lax.map(block, qb)
    return o.transpose(1, 0, 2, 3, 4).reshape(B, T, MLA_WIDTH)


def neighbourhood_attention(q, k, v, rpb):
    B, T, _ = q.shape
    R = T // GRID_W
    kh = min(NA_KH_MAX, R)
    q = q.reshape(B, T, NA_HEADS, NA_HEAD_DIM)
    k = k.reshape(B, T, NA_HEADS, NA_HEAD_DIM)
    v = v.reshape(B, T, NA_HEADS, NA_HEAD_DIM)
    cols = jnp.arange(GRID_W)
    col_idx = jnp.clip(cols - NA_KW // 2, 0, GRID_W - NA_KW)[:, None] + jnp.arange(NA_KW)[None, :]
    dc = col_idx - cols[:, None]
    scale = NA_HEAD_DIM ** -0.5
    q_rows = q.reshape(B, R, GRID_W, NA_HEADS, NA_HEAD_DIM).transpose(1, 0, 2, 3, 4)

    def row(args):
        qi, i = args
        rows = jnp.clip(i - kh // 2, 0, R - kh) + jnp.arange(kh)
        key_idx = (rows[None, :, None] * GRID_W + col_idx[:, None, :]).reshape(GRID_W, kh * NA_KW)
        kg = jnp.take(k, key_idx, axis=1)
        vg = jnp.take(v, key_idx, axis=1)
        bias = rpb[:, (rows - i)[None, :, None] + NA_KH_MAX - 1, dc[:, None, :] + NA_KW - 1]
        bias = bias.reshape(NA_HEADS, GRID_W, kh * NA_KW).astype(jnp.float32)
        s = jnp.einsum('bqhd,bqkhd->bhqk', qi, kg).astype(jnp.float32) * scale + bias[None]
        p = jax.nn.softmax(s, axis=-1).astype(v.dtype)
        return jnp.einsum('bhqk,bqkhd->bqhd', p, vg)

    o = lax.map(row, (q_rows, jnp.arange(R)))
    return o.transpose(1, 0, 2, 3, 4).reshape(B, T, NA_WIDTH)


def moe(xf, w_router, b_router, w_gate, b_gate, w_up, b_up, w_down, b_down):
    N, D = xf.shape
    A = N * TOP_K
    nblk = -(-A // MOE_BLOCK) + N_EXPERTS
    P = nblk * MOE_BLOCK
    logits = (xf @ w_router).astype(jnp.float32) + b_router.astype(jnp.float32)
    top_v, top_i = lax.top_k(logits, TOP_K)
    gates = jax.nn.softmax(top_v, axis=-1).astype(xf.dtype)
    flat_e = top_i.reshape(-1)
    flat_tok = jnp.arange(A, dtype=jnp.int32) // TOP_K
    flat_g = gates.reshape(-1)
    order = jnp.argsort(flat_e, stable=True)
    se = flat_e[order]
    counts = jnp.bincount(flat_e, length=N_EXPERTS)
    padded = ((counts + MOE_BLOCK - 1) // MOE_BLOCK) * MOE_BLOCK
    pad_end = jnp.cumsum(padded)
    pad_start = pad_end - padded
    start = jnp.cumsum(counts) - counts
    dest = pad_start[se] + jnp.arange(A) - start[se]
    tok_buf = jnp.full((P,), N, dtype=jnp.int32).at[dest].set(flat_tok[order])
    gate_buf = jnp.zeros((P,), xf.dtype).at[dest].set(flat_g[order])
    blk_e = jnp.minimum(jnp.searchsorted(pad_end, jnp.arange(nblk) * MOE_BLOCK, side='right'), N_EXPERTS - 1)
    x_pad = jnp.concatenate([xf, jnp.zeros((1, D), xf.dtype)], axis=0)

    def expert_block(args):
        idx, e = args
        xb = x_pad[idx]
        g = xb @ w_gate[e] + b_gate[e]
        u = xb @ w_up[e] + b_up[e]
        g = jnp.minimum(g, SWIGLU_LIMIT)
        u = jnp.clip(u, -SWIGLU_LIMIT, SWIGLU_LIMIT)
        h = (u + 1.0) * (g * jax.nn.sigmoid(SWIGLU_ALPHA * g))
        return h @ w_down[e] + b_down[e]

    ys = lax.map(expert_block, (tok_buf.reshape(nblk, MOE_BLOCK), blk_e))
    ys = ys.reshape(P, D) * gate_buf[:, None]
    return jax.ops.segment_sum(ys, tok_buf, num_segments=N + 1)[:N]


def encoder_layer(x, g_attn_norm, w_in, g_q_lora, w_uq, g_kv_lora, w_ukv, na_rpb,
                  g_out_mla, g_out_na, w_out, g_ffn_norm, w_router, b_router,
                  w_gate, b_gate, w_up, b_up, w_down, b_down):
    B, T, D = x.shape
    h = rms_norm(x, g_attn_norm)
    proj = h @ w_in
    offs = list(np.cumsum(IN_SPLITS)[:-1])
    c_q, c_kv, k_pe, na_q, na_k, na_v = jnp.split(proj, offs, axis=-1)
    a = mla_attention(c_q, c_kv, k_pe, g_q_lora, w_uq, g_kv_lora, w_ukv)
    n = neighbourhood_attention(na_q, na_k, na_v, na_rpb)
    mixed = jnp.concatenate([rms_norm(a, g_out_mla), rms_norm(n, g_out_na)], axis=-1) @ w_out
    x = x + mixed
    hf = rms_norm(x, g_ffn_norm).reshape(B * T, D)
    x = x + moe(hf, w_router, b_router, w_gate, b_gate, w_up, b_up, w_down, b_down).reshape(B, T, D)
    return x


def trunk(x, g_attn_norm, w_in, g_q_lora, w_uq, g_kv_lora, w_ukv, na_rpb,
          g_out_mla, g_out_na, w_out, g_ffn_norm, w_router, b_router,
          w_gate, b_gate, w_up, b_up, w_down, b_down, g_final):
    for l in range(DEPTH):
        x = encoder_layer(x, g_attn_norm[l], w_in[l], g_q_lora[l], w_uq[l], g_kv_lora[l], w_ukv[l],
                          na_rpb[l], g_out_mla[l], g_out_na[l], w_out[l], g_ffn_norm[l],
                          w_router[l], b_router[l], w_gate[l], b_gate[l], w_up[l], b_up[l],
                          w_down[l], b_down[l])
    return rms_norm(x, g_final)


def setup_inputs(seed: int = 0) -> dict:
    key = jax.random.key(seed)
    ks = jax.random.split(key, 24)
    L = DEPTH
    f32 = jnp.float32

    def nrm(k, shape, scale):
        return jax.random.normal(k, shape, f32) * scale

    def gain(k, shape):
        return 1.0 + 0.01 * jax.random.normal(k, shape, f32)

    return {
        "x_prompt": jax.random.normal(ks[0], (BATCH, SEQ, D_MODEL), f32),
        "x_sample": jax.random.normal(ks[1], (DEC_BATCH, DEC_SEQ, D_MODEL), f32),
        "g_attn_norm": gain(ks[2], (L, D_MODEL)),
        "w_in": nrm(ks[3], (L, D_MODEL, IN_DIM), D_MODEL ** -0.5),
        "g_q_lora": gain(ks[4], (L, Q_LORA)),
        "w_uq": nrm(ks[5], (L, Q_LORA, MLA_HEADS * QK_HEAD), Q_LORA ** -0.5),
        "g_kv_lora": gain(ks[6], (L, KV_LORA)),
        "w_ukv": nrm(ks[7], (L, KV_LORA, MLA_HEADS * (QK_NOPE + V_HEAD)), KV_LORA ** -0.5),
        "na_rpb": nrm(ks[8], (L, NA_HEADS, 2 * NA_KH_MAX - 1, 2 * NA_KW - 1), 0.02),
        "g_out_mla": gain(ks[9], (L, MLA_WIDTH)),
        "g_out_na": gain(ks[10], (L, NA_WIDTH)),
        "w_out": nrm(ks[11], (L, MIX_WIDTH, D_MODEL), MIX_WIDTH ** -0.5),
        "g_ffn_norm": gain(ks[12], (L, D_MODEL)),
        "w_router": nrm(ks[13], (L, D_MODEL, N_EXPERTS), D_MODEL ** -0.5),
        "b_router": nrm(ks[14], (L, N_EXPERTS), 0.01),
        "w_gate": nrm(ks[15], (L, N_EXPERTS, D_MODEL, D_FF), D_MODEL ** -0.5),
        "b_gate": nrm(ks[16], (L, N_EXPERTS, D_FF), 0.01),
        "w_up": nrm(ks[17], (L, N_EXPERTS, D_MODEL, D_FF), D_MODEL ** -0.5),
        "b_up": nrm(ks[18], (L, N_EXPERTS, D_FF), 0.01),
        "w_down": nrm(ks[19], (L, N_EXPERTS, D_FF, D_MODEL), D_FF ** -0.5),
        "b_down": nrm(ks[20], (L, N_EXPERTS, D_MODEL), 0.01),
        "g_final": gain(ks[21], (D_MODEL,)),
    }


def reference(x_prompt, x_sample, g_attn_norm, w_in, g_q_lora, w_uq, g_kv_lora, w_ukv, na_rpb,
              g_out_mla, g_out_na, w_out, g_ffn_norm, w_router, b_router,
              w_gate, b_gate, w_up, b_up, w_down, b_down, g_final):
    y_prompt = trunk(x_prompt, g_attn_norm, w_in, g_q_lora, w_uq, g_kv_lora, w_ukv, na_rpb,
                     g_out_mla, g_out_na, w_out, g_ffn_norm, w_router, b_router,
                     w_gate, b_gate, w_up, b_up, w_down, b_down, g_final)
    y_sample = trunk(x_sample, g_attn_norm, w_in, g_q_lora, w_uq, g_kv_lora, w_ukv, na_rpb,
                     g_out_mla, g_out_na, w_out, g_ffn_norm, w_router, b_router,
                     w_gate, b_gate, w_up, b_up, w_down, b_down, g_final)
    return (y_prompt, y_sample)
```

```python
import functools

import numpy as np
import jax
import jax.numpy as jnp
from jax import lax
from jax.experimental import pallas as pl
from jax.experimental.pallas import tpu as pltpu

F32 = jnp.float32
BF16 = jnp.bfloat16

EPS = 1e-6
MLA_HEADS = 8
Q_LORA = 256
KV_LORA = 128
QK_NOPE = 64
ROPE_DIM = 32
QK_HEAD = QK_NOPE + ROPE_DIM
V_HEAD = 64
ROPE_THETA = 10000.0
NA_HEADS = 8
NA_HEAD_DIM = 64
NA_WIDTH = NA_HEADS * NA_HEAD_DIM
MLA_WIDTH = MLA_HEADS * V_HEAD
GRID_W = 64
NA_KH = 8
NA_KW = 16
N_EXPERTS = 32
TOP_K = 4
SWIGLU_LIMIT = 7.0
SWIGLU_ALPHA = 1.702

LANE = 128
VMEM_LIMIT = 56 * 1024 * 1024

TOK_TILE = 512
MLA_TQ = 512
MLA_TK = 512
NA_RQ = 4
NA_WROWS = 12
MOE_BM = 512
NEG = -1e30

IN_COLS = 3 * NA_WIDTH + Q_LORA + KV_LORA + 2 * LANE


def _rms(x):
    return x * lax.rsqrt(jnp.mean(x * x, axis=-1, keepdims=True) + EPS)


def _dot(a, b):
    return jnp.dot(a, b, preferred_element_type=F32)


def _dot_nt(a, b):
    return lax.dot_general(a, b, (((1,), (1,)), ((), ())), preferred_element_type=F32)


def _cparams(sem):
    return pltpu.CompilerParams(dimension_semantics=sem, vmem_limit_bytes=VMEM_LIMIT)


def _proj_kernel(x_ref, g_ref, win_ref, gq_ref, wqa_ref, wqb_ref, gkv_ref, wuk_ref, wuv_ref, tab_ref,
                 q_ref, k_ref, v_ref, naq_ref, nak_ref, nav_ref):
    h = (_rms(x_ref[...]) * g_ref[...]).astype(BF16)
    proj = _dot(h, win_ref[...])
    o = 0
    naq_ref[...] = (proj[:, o:o + NA_WIDTH] * (NA_HEAD_DIM ** -0.5)).astype(BF16)
    o += NA_WIDTH
    nak_ref[...] = proj[:, o:o + NA_WIDTH].astype(BF16)
    o += NA_WIDTH
    nav_ref[...] = proj[:, o:o + NA_WIDTH].astype(BF16)
    o += NA_WIDTH
    cq = (_rms(proj[:, o:o + Q_LORA]) * gq_ref[...]).astype(BF16)
    o += Q_LORA
    ckv = (_rms(proj[:, o:o + KV_LORA]) * gkv_ref[...]).astype(BF16)
    o += KV_LORA
    kpe_a = proj[:, o:o + LANE]
    kpe_b = proj[:, o + LANE:o + 2 * LANE]

    tab = tab_ref[...]
    caq, sbq = tab[:, 0:LANE], tab[:, LANE:2 * LANE]
    cak, sbk = tab[:, 2 * LANE:3 * LANE], tab[:, 3 * LANE:4 * LANE]

    qa = _dot(cq, wqa_ref[...])
    qb = _dot(cq, wqb_ref[...])
    kn = _dot(ckv, wuk_ref[...])
    kpe = kpe_a * cak + kpe_b * sbk
    for hd in range(MLA_HEADS):
        sl = slice(hd * LANE, (hd + 1) * LANE)
        q_ref[:, sl] = (qa[:, sl] * caq + qb[:, sl] * sbq).astype(BF16)
        k_ref[:, sl] = (kn[:, sl] + kpe).astype(BF16)
    v_ref[...] = _dot(ckv, wuv_ref[...]).astype(BF16)


def _proj_call(x, g_attn, win_r, g_q, wqa, wqb, g_kv, wuk, wuv, tab, pos_blocks):
    n, d = x.shape
    tm = TOK_TILE
    (n0, pb0), (_, pb1) = pos_blocks

    def tab_map(i):
        return (jnp.where(i < n0, i % pb0, (i - n0) % pb1), 0)

    row = lambda i: (i, 0)
    fix = lambda i: (0, 0)
    out_w = (MLA_HEADS * LANE, MLA_HEADS * LANE, MLA_WIDTH, NA_WIDTH, NA_WIDTH, NA_WIDTH)
    return pl.pallas_call(
        _proj_kernel,
        grid=(n // tm,),
        in_specs=[
            pl.BlockSpec((tm, d), row),
            pl.BlockSpec((1, d), fix),
            pl.BlockSpec(win_r.shape, fix),
            pl.BlockSpec((1, Q_LORA), fix),
            pl.BlockSpec(wqa.shape, fix),
            pl.BlockSpec(wqb.shape, fix),
            pl.BlockSpec((1, KV_LORA), fix),
            pl.BlockSpec(wuk.shape, fix),
            pl.BlockSpec(wuv.shape, fix),
            pl.BlockSpec((tm, 4 * LANE), tab_map),
        ],
        out_specs=[pl.BlockSpec((tm, w), row) for w in out_w],
        out_shape=[jax.ShapeDtypeStruct((n, w), BF16) for w in out_w],
        compiler_params=_cparams(("parallel",)),
        name="proj",
    )(x, g_attn, win_r, g_q, wqa, wqb, g_kv, wuk, wuv, tab)


def _mla_kernel(q_ref, k_ref, v_ref, o_ref, *, tk):
    tq = q_ref.shape[1]
    nchunk = k_ref.shape[1] // tk
    qs = (q_ref[0, :, 0:LANE], q_ref[0, :, LANE:2 * LANE])

    def body(c, carry):
        off = pl.multiple_of(c * tk, tk)
        vc = v_ref[0, pl.ds(off, tk), :]
        new = []
        for hh in range(2):
            m, l, acc = carry[hh]
            kc = k_ref[0, pl.ds(off, tk), hh * LANE:(hh + 1) * LANE]
            s = _dot_nt(qs[hh], kc)
            mn = jnp.maximum(m, jnp.max(s, axis=-1, keepdims=True))
            alpha = jnp.exp(m - mn)
            p = jnp.exp(s - mn)
            l = alpha * l + jnp.sum(p, axis=-1, keepdims=True)
            acc = alpha * acc + _dot(p.astype(BF16), vc)
            new.append((mn, l, acc))
        return tuple(new)

    init = tuple((jnp.full((tq, 1), NEG, F32), jnp.zeros((tq, 1), F32), jnp.zeros((tq, LANE), F32))
                 for _ in range(2))
    (_, l0, a0), (_, l1, a1) = lax.fori_loop(0, nchunk, body, init)
    lane = lax.broadcasted_iota(jnp.int32, (tq, LANE), 1)
    o_ref[0] = jnp.where(lane < V_HEAD, a0 / l0, a1 / l1).astype(o_ref.dtype)


def _mla_call(q, k, v, nb, t, boff):
    tq = min(MLA_TQ, t)
    tk = min(MLA_TK, t)
    return pl.pallas_call(
        functools.partial(_mla_kernel, tk=tk),
        grid=(nb, MLA_HEADS // 2, t // tq),
        in_specs=[
            pl.BlockSpec((1, tq, 2 * LANE), lambda b, p, i: (b + boff, i, p)),
            pl.BlockSpec((1, t, 2 * LANE), lambda b, p, i: (b + boff, 0, p)),
            pl.BlockSpec((1, t, LANE), lambda b, p, i: (b + boff, 0, p)),
        ],
        out_specs=pl.BlockSpec((1, tq, LANE), lambda b, p, i: (b, i, p)),
        out_shape=jax.ShapeDtypeStruct((nb, t, MLA_WIDTH), BF16),
        compiler_params=_cparams(("parallel", "parallel", "arbitrary")),
        name="mla",
    )(q, k, v)


def _na_base(i, rows):
    return jnp.clip(i * NA_RQ - NA_KH // 2, 0, rows - NA_WROWS)


def _na_kernel(q_ref, k_ref, v_ref, b_ref, o_ref):
    nq = q_ref.shape[1]
    rows = k_ref.shape[1] // GRID_W
    nk = NA_WROWS * GRID_W
    off = pl.multiple_of(_na_base(pl.program_id(1), rows) * GRID_W, GRID_W)
    lane = lax.broadcasted_iota(jnp.int32, (nq, LANE), 1)
    low = lane < NA_HEAD_DIM
    for p in range(NA_HEADS // 2):
        sl = slice(p * LANE, (p + 1) * LANE)
        q2 = q_ref[0, :, sl]
        kw = k_ref[0, pl.ds(off, nk), sl]
        vw = v_ref[0, pl.ds(off, nk), sl]
        outs = []
        for hh in range(2):
            qm = jnp.where(low, q2, 0) if hh == 0 else jnp.where(low, 0, q2)
            s = _dot_nt(qm.astype(BF16), kw) + b_ref[0, 2 * p + hh]
            m = jnp.max(s, axis=-1, keepdims=True)
            e = jnp.exp(s - m)
            l = jnp.sum(e, axis=-1, keepdims=True)
            outs.append(_dot(e.astype(BF16), vw) / l)
        o_ref[0, :, sl] = jnp.where(low, outs[0], outs[1]).astype(o_ref.dtype)


def _na_call(q, k, v, bias, nb, t, boff):
    rows = t // GRID_W
    nq = NA_RQ * GRID_W
    nk = NA_WROWS * GRID_W

    def bias_map(b, i):
        return ((i * NA_RQ - _na_base(i, rows)) // (NA_KH // 2), 0, 0, 0)

    return pl.pallas_call(
        _na_kernel,
        grid=(nb, rows // NA_RQ),
        in_specs=[
            pl.BlockSpec((1, nq, NA_WIDTH), lambda b, i: (b + boff, i, 0)),
            pl.BlockSpec((1, t, NA_WIDTH), lambda b, i: (b + boff, 0, 0)),
            pl.BlockSpec((1, t, NA_WIDTH), lambda b, i: (b + boff, 0, 0)),
            pl.BlockSpec((1, NA_HEADS, nq, nk), bias_map),
        ],
        out_specs=pl.BlockSpec((1, nq, NA_WIDTH), lambda b, i: (b, i, 0)),
        out_shape=jax.ShapeDtypeStruct((nb, t, NA_WIDTH), BF16),
        compiler_params=_cparams(("parallel", "arbitrary")),
        name="natten",
    )(q, k, v, bias)


def _na_bias_table(rpb):
    rows = 32
    tabs = []
    for i0 in (0, NA_RQ, rows - NA_RQ):
        base = int(np.clip(i0 - NA_KH // 2, 0, rows - NA_WROWS))
        qi = i0 + np.arange(NA_RQ)
        start = np.clip(qi - NA_KH // 2, 0, rows - NA_KH)
        kr = base + np.arange(NA_WROWS)
        row_ok = (kr[None, :] >= start[:, None]) & (kr[None, :] < start[:, None] + NA_KH)
        ridx = np.clip(kr[None, :] - qi[:, None] + NA_KH - 1, 0, 2 * NA_KH - 2)
        c = np.arange(GRID_W)
        cstart = np.clip(c - NA_KW // 2, 0, GRID_W - NA_KW)
        col_ok = (c[None, :] >= cstart[:, None]) & (c[None, :] < cstart[:, None] + NA_KW)
        cidx = np.clip(c[None, :] - c[:, None] + NA_KW - 1, 0, 2 * NA_KW - 2)
        ok = row_ok[:, None, :, None] & col_ok[None, :, None, :]
        ri = np.broadcast_to(ridx[:, None, :, None], ok.shape)
        ci = np.broadcast_to(cidx[None, :, None, :], ok.shape)
        vals = rpb[:, ri, ci].astype(F32)
        tab = jnp.where(ok[None], vals, NEG)
        tabs.append(tab.reshape(NA_HEADS, NA_RQ * GRID_W, NA_WROWS * GRID_W))
    return jnp.stack(tabs)


def _mix_kernel(a_ref, n_ref, x_ref, gm_ref, gn_ref, wo_ref, gf_ref, wrh_ref, wrl_ref, br_ref,
                x2_ref, hf_ref, lg_ref):
    an = _rms(a_ref[...].astype(F32)) * gm_ref[...]
    nn = _rms(n_ref[...].astype(F32)) * gn_ref[...]
    cat = jnp.concatenate([an, nn], axis=-1).astype(BF16)
    x2 = x_ref[...] + _dot(cat, wo_ref[...])
    x2_ref[...] = x2
    hf = _rms(x2) * gf_ref[...]
    hi = hf.astype(BF16)
    hf_ref[...] = hi
    lo = (hf - hi.astype(F32)).astype(BF16)
    lg_ref[...] = (_dot(hi, wrh_ref[...]) + (_dot(lo, wrh_ref[...]) + _dot(hi, wrl_ref[...]))) + br_ref[...]


def _mix_call(a, nat, x, g_mla, g_na, w_out, g_ffn, wr_hi, wr_lo, b_r):
    n, d = x.shape
    tm = TOK_TILE
    row = lambda i: (i, 0)
    fix = lambda i: (0, 0)
    return pl.pallas_call(
        _mix_kernel,
        grid=(n // tm,),
        in_specs=[
            pl.BlockSpec((tm, MLA_WIDTH), row),
            pl.BlockSpec((tm, NA_WIDTH), row),
            pl.BlockSpec((tm, d), row),
            pl.BlockSpec((1, MLA_WIDTH), fix),
            pl.BlockSpec((1, NA_WIDTH), fix),
            pl.BlockSpec(w_out.shape, fix),
            pl.BlockSpec((1, d), fix),
            pl.BlockSpec(wr_hi.shape, fix),
            pl.BlockSpec(wr_lo.shape, fix),
            pl.BlockSpec((1, LANE), fix),
        ],
        out_specs=[pl.BlockSpec((tm, d), row), pl.BlockSpec((tm, d), row), pl.BlockSpec((tm, LANE), row)],
        out_shape=[jax.ShapeDtypeStruct((n, d), F32), jax.ShapeDtypeStruct((n, d), BF16),
                   jax.ShapeDtypeStruct((n, LANE), F32)],
        compiler_params=_cparams(("parallel",)),
        name="mix",
    )(a, nat, x, g_mla, g_na, w_out, g_ffn, wr_hi, wr_lo, b_r)


def _moe_kernel(be_ref, nu_ref, x_ref, gt_ref, wg_ref, bg_ref, wu_ref, bu_ref, wd_ref, bd_ref, y_ref):
    i = pl.program_id(0)

    @pl.when(i < nu_ref[0])
    def _():
        x = x_ref[...]
        g = _dot(x, wg_ref[0]) + bg_ref[0]
        u = _dot(x, wu_ref[0]) + bu_ref[0]
        g = jnp.minimum(g, SWIGLU_LIMIT)
        u = jnp.clip(u, -SWIGLU_LIMIT, SWIGLU_LIMIT)
        h = (u + 1.0) * (g * jax.nn.sigmoid(SWIGLU_ALPHA * g))
        y = _dot(h.astype(BF16), wd_ref[0]) + bd_ref[0]
        gate = gt_ref[...]
        for j in range(y.shape[1] // LANE):
            sl = slice(j * LANE, (j + 1) * LANE)
            y_ref[:, sl] = y[:, sl] * gate

    @pl.when(i >= nu_ref[0])
    def _():
        y_ref[...] = jnp.zeros_like(y_ref)


def _moe_call(blk_e, nused, xs, gates, wg, bg, wu, bu, wd, bd):
    p, d = xs.shape
    bm = MOE_BM
    dff = wg.shape[2]

    def xmap(i, be, nu):
        return (jnp.minimum(i, nu[0] - 1), 0)

    wmap = lambda i, be, nu: (be[i], 0, 0)
    grid_spec = pltpu.PrefetchScalarGridSpec(
        num_scalar_prefetch=2,
        grid=(p // bm,),
        in_specs=[
            pl.BlockSpec((bm, d), xmap),
            pl.BlockSpec((bm, LANE), xmap),
            pl.BlockSpec((1, d, dff), wmap),
            pl.BlockSpec((1, 1, dff), wmap),
            pl.BlockSpec((1, d, dff), wmap),
            pl.BlockSpec((1, 1, dff), wmap),
            pl.BlockSpec((1, dff, d), wmap),
            pl.BlockSpec((1, 1, d), wmap),
        ],
        out_specs=pl.BlockSpec((bm, d), lambda i, be, nu: (i, 0)),
    )
    return pl.pallas_call(
        _moe_kernel,
        grid_spec=grid_spec,
        out_shape=jax.ShapeDtypeStruct((p, d), F32),
        compiler_params=_cparams(("arbitrary",)),
        name="moe",
    )(blk_e, nused, xs, gates, wg, bg, wu, bu, wd, bd)


def _final_kernel(x_ref, m_ref, g_ref, o_ref):
    o_ref[...] = _rms(x_ref[...] + m_ref[...]) * g_ref[...]


def _final_call(x2, moe_out, g_final):
    n, d = x2.shape
    tm = TOK_TILE
    row = lambda i: (i, 0)
    return pl.pallas_call(
        _final_kernel,
        grid=(n // tm,),
        in_specs=[pl.BlockSpec((tm, d), row), pl.BlockSpec((tm, d), row), pl.BlockSpec((1, d), lambda i: (0, 0))],
        out_specs=pl.BlockSpec((tm, d), row),
        out_shape=jax.ShapeDtypeStruct((n, d), F32),
        compiler_params=_cparams(("parallel",)),
        name="final",
    )(x2, moe_out, g_final)


def _prep_weights(w_in, w_uq, w_ukv, t_max):
    d = w_in.shape[0]
    o = np.cumsum((0, Q_LORA, KV_LORA, ROPE_DIM, NA_WIDTH, NA_WIDTH, NA_WIDTH))
    w_cq, w_ckv, w_kpe = w_in[:, o[0]:o[1]], w_in[:, o[1]:o[2]], w_in[:, o[2]:o[3]]
    w_naq, w_nak, w_nav = w_in[:, o[3]:o[4]], w_in[:, o[4]:o[5]], w_in[:, o[5]:o[6]]
    half = ROPE_DIM // 2
    swap = np.concatenate([np.arange(half, ROPE_DIM), np.arange(half)])
    zpad = lambda w, lo: jnp.pad(w, ((0, 0), (lo, LANE - lo - w.shape[1])))
    win_r = jnp.concatenate([w_naq, w_nak, w_nav, w_cq, w_ckv,
                             zpad(w_kpe, QK_NOPE), zpad(w_kpe[:, swap], QK_NOPE)], axis=1).astype(BF16)

    uq = w_uq.reshape(Q_LORA, MLA_HEADS, QK_HEAD)
    wqa = jnp.pad(uq, ((0, 0), (0, 0), (0, LANE - QK_HEAD))).reshape(Q_LORA, MLA_HEADS * LANE).astype(BF16)
    wqb = jnp.pad(uq[:, :, QK_NOPE:][:, :, swap], ((0, 0), (0, 0), (QK_NOPE, LANE - QK_HEAD)))
    wqb = wqb.reshape(Q_LORA, MLA_HEADS * LANE).astype(BF16)
    ukv = w_ukv.reshape(KV_LORA, MLA_HEADS, QK_NOPE + V_HEAD)
    wuk = jnp.pad(ukv[:, :, :QK_NOPE], ((0, 0), (0, 0), (0, LANE - QK_NOPE)))
    wuk = wuk.reshape(KV_LORA, MLA_HEADS * LANE).astype(BF16)
    wuv = ukv[:, :, QK_NOPE:].reshape(KV_LORA, MLA_WIDTH).astype(BF16)

    pos = jnp.arange(t_max, dtype=F32)
    inv = 1.0 / (ROPE_THETA ** (jnp.arange(0, ROPE_DIM, 2, dtype=F32) / ROPE_DIM))
    ang = pos[:, None] * inv[None, :]
    cos, sin = jnp.cos(ang), jnp.sin(ang)
    c32 = jnp.concatenate([cos, cos], axis=1)
    s32 = jnp.concatenate([-sin, sin], axis=1)
    z = lambda w: jnp.zeros((t_max, w), F32)
    cak = jnp.concatenate([z(QK_NOPE), c32, z(LANE - QK_HEAD)], axis=1)
    sbk = jnp.concatenate([z(QK_NOPE), s32, z(LANE - QK_HEAD)], axis=1)
    scale = QK_HEAD ** -0.5
    caq = scale * jnp.concatenate([jnp.ones((t_max, QK_NOPE), F32), c32, z(LANE - QK_HEAD)], axis=1)
    tab = jnp.concatenate([caq, scale * sbk, cak, sbk], axis=1)
    del d
    return win_r, wqa, wqb, wuk, wuv, tab


def _route(logits, bm):
    n = logits.shape[0]
    a = n * TOP_K
    nblk = a // bm + N_EXPERTS
    top_v, top_i = lax.top_k(logits, TOP_K)
    gates = jax.nn.softmax(top_v, axis=-1)
    flat_e = top_i.reshape(-1).astype(jnp.int32)
    order = jnp.argsort(flat_e, stable=True).astype(jnp.int32)
    se = flat_e[order]
    edges = jnp.searchsorted(se, jnp.arange(N_EXPERTS + 1, dtype=jnp.int32), side="left").astype(jnp.int32)
    start, counts = edges[:-1], edges[1:] - edges[:-1]
    padded = ((counts + bm - 1) // bm) * bm
    pad_end = jnp.cumsum(padded)
    pad_start = pad_end - padded
    blk_e = jnp.minimum(jnp.searchsorted(pad_end, jnp.arange(nblk, dtype=jnp.int32) * bm, side="right"),
                        N_EXPERTS - 1).astype(jnp.int32)
    nused = (pad_end[-1:] // bm).astype(jnp.int32)
    slot = jnp.arange(nblk * bm, dtype=jnp.int32)
    se_slot = jnp.repeat(blk_e, bm)
    rank = slot - pad_start[se_slot]
    valid = (rank < counts[se_slot]) & (slot < pad_end[-1])
    src = order[jnp.clip(start[se_slot] + rank, 0, a - 1)]
    tok_buf = jnp.where(valid, src // TOP_K, n)
    gate_buf = jnp.where(valid, gates.reshape(-1)[src], 0.0)
    dest = pad_start[se] + jnp.arange(a, dtype=jnp.int32) - start[se]
    pos = jnp.zeros((a,), jnp.int32).at[order].set(dest, unique_indices=True)
    return tok_buf, gate_buf, pos.reshape(n, TOP_K), blk_e, nused


def kernel(x_prompt, x_sample, g_attn_norm, w_in, g_q_lora, w_uq, g_kv_lora, w_ukv, na_rpb, g_out_mla,
           g_out_na, w_out, g_ffn_norm, w_router, b_router, w_gate, b_gate, w_up, b_up, w_down, b_down,
           g_final):
    assert g_attn_norm.shape[0] == 1, "single-layer trunk"
    d = x_prompt.shape[-1]
    groups = [x_prompt, x_sample]
    shapes = [(g.shape[0], g.shape[1]) for g in groups]
    for _, t in shapes:
        assert t % TOK_TILE == 0 and t % GRID_W == 0 and (t // GRID_W) % NA_RQ == 0
        assert t // GRID_W >= NA_WROWS + NA_RQ
    x = jnp.concatenate([g.reshape(-1, d) for g in groups], axis=0)
    n = x.shape[0]
    t_max = max(t for _, t in shapes)

    win_r, wqa, wqb, wuk, wuv, tab = _prep_weights(w_in[0], w_uq[0], w_ukv[0], t_max)
    pos_blocks = [(b * t // TOK_TILE, t // TOK_TILE) for b, t in shapes]
    row = lambda v: v.reshape(1, -1)
    q, k, v, naq, nak, nav = _proj_call(x, row(g_attn_norm[0]), win_r, row(g_q_lora[0]), wqa, wqb,
                                        row(g_kv_lora[0]), wuk, wuv, tab, pos_blocks)

    bias = _na_bias_table(na_rpb[0])
    a_parts, n_parts = [], []
    tok0 = 0
    for b, t in shapes:
        assert tok0 % t == 0
        boff = tok0 // t
        view = lambda arr: arr.reshape(n // t, t, arr.shape[-1])
        a_parts.append(_mla_call(view(q), view(k), view(v), b, t, boff).reshape(b * t, MLA_WIDTH))
        n_parts.append(_na_call(view(naq), view(nak), view(nav), bias, b, t, boff).reshape(b * t, NA_WIDTH))
        tok0 += b * t
    a_all = jnp.concatenate(a_parts, axis=0)
    n_all = jnp.concatenate(n_parts, axis=0)

    wr = jnp.pad(w_router[0], ((0, 0), (0, LANE - N_EXPERTS)))
    wr_hi = wr.astype(BF16)
    wr_lo = (wr - wr_hi.astype(F32)).astype(BF16)
    b_r = jnp.pad(b_router[0], (0, LANE - N_EXPERTS)).reshape(1, LANE)
    x2, hf, logits = _mix_call(a_all, n_all, x, row(g_out_mla[0]), row(g_out_na[0]), w_out[0].astype(BF16),
                               row(g_ffn_norm[0]), wr_hi, wr_lo, b_r)

    tok_buf, gate_buf, pos, blk_e, nused = _route(logits[:, :N_EXPERTS], MOE_BM)
    hf_pad = jnp.concatenate([hf, jnp.zeros((1, d), hf.dtype)], axis=0)
    xs = hf_pad[tok_buf]
    gates_b = jnp.broadcast_to(gate_buf[:, None], (gate_buf.shape[0], LANE))
    ys = _moe_call(blk_e, nused, xs, gates_b,
                   w_gate[0].astype(BF16), b_gate[0][:, None, :], w_up[0].astype(BF16), b_up[0][:, None, :],
                   w_down[0].astype(BF16), b_down[0][:, None, :])
    moe_out = ys[pos.reshape(-1)].reshape(n, TOP_K, d).sum(axis=1)

    y = _final_call(x2, moe_out, row(g_final))
    outs, tok0 = [], 0
    for b, t in shapes:
        outs.append(y[tok0:tok0 + b * t].reshape(b, t, d))
        tok0 += b * t
    return tuple(outs)
```

```python
import functools

import numpy as np
import jax
import jax.numpy as jnp
from jax import lax
from jax.experimental import pallas as pl
from jax.experimental.pallas import tpu as pltpu

F32 = jnp.float32
BF16 = jnp.bfloat16

EPS = 1e-6
MLA_HEADS = 8
Q_LORA = 256
KV_LORA = 128
QK_NOPE = 64
ROPE_DIM = 32
QK_HEAD = QK_NOPE + ROPE_DIM
V_HEAD = 64
ROPE_THETA = 10000.0
NA_HEADS = 8
NA_HEAD_DIM = 64
NA_WIDTH = NA_HEADS * NA_HEAD_DIM
MLA_WIDTH = MLA_HEADS * V_HEAD
GRID_W = 64
NA_KH = 8
NA_KW = 16
N_EXPERTS = 32
TOP_K = 4
SWIGLU_LIMIT = 7.0
SWIGLU_ALPHA = 1.702

LANE = 128
SUB = 8
VMEM_LIMIT = 56 * 1024 * 1024

TOK_TILE = 512
MLA_TQ = 512
MLA_TK = 4096
NA_RQ = 4
NA_WROWS = 12
MOE_BM = 512
NEG = -1e30

IN_COLS = 3 * NA_WIDTH + Q_LORA + KV_LORA + 2 * LANE


def _rms(x):
    return x * lax.rsqrt(jnp.mean(x * x, axis=-1, keepdims=True) + EPS)


def _dot(a, b):
    return jnp.dot(a, b, preferred_element_type=F32)


def _dot_nt(a, b):
    return lax.dot_general(a, b, (((1,), (1,)), ((), ())), preferred_element_type=F32)


def _cparams(sem):
    return pltpu.CompilerParams(dimension_semantics=sem, vmem_limit_bytes=VMEM_LIMIT)


def _proj_kernel(x_ref, g_ref, win_ref, gq_ref, wqa_ref, wqb_ref, gkv_ref, wuk_ref, wuv_ref, tab_ref,
                 q_ref, k_ref, v_ref, naq_ref, nak_ref, nav_ref):
    h = (_rms(x_ref[...]) * g_ref[...]).astype(BF16)
    proj = _dot(h, win_ref[...])
    o = 0
    naq_ref[...] = (proj[:, o:o + NA_WIDTH] * (NA_HEAD_DIM ** -0.5)).astype(BF16)
    o += NA_WIDTH
    nak_ref[...] = proj[:, o:o + NA_WIDTH].astype(BF16)
    o += NA_WIDTH
    nav_ref[...] = proj[:, o:o + NA_WIDTH].astype(BF16)
    o += NA_WIDTH
    cq = (_rms(proj[:, o:o + Q_LORA]) * gq_ref[...]).astype(BF16)
    o += Q_LORA
    ckv = (_rms(proj[:, o:o + KV_LORA]) * gkv_ref[...]).astype(BF16)
    o += KV_LORA
    kpe_a = proj[:, o:o + LANE]
    kpe_b = proj[:, o + LANE:o + 2 * LANE]

    tab = tab_ref[...]
    caq, sbq = tab[:, 0:LANE], tab[:, LANE:2 * LANE]
    cak, sbk = tab[:, 2 * LANE:3 * LANE], tab[:, 3 * LANE:4 * LANE]

    qa = _dot(cq, wqa_ref[...])
    qb = _dot(cq, wqb_ref[...])
    kn = _dot(ckv, wuk_ref[...])
    kpe = kpe_a * cak + kpe_b * sbk
    for hd in range(MLA_HEADS):
        sl = slice(hd * LANE, (hd + 1) * LANE)
        q_ref[:, sl] = (qa[:, sl] * caq + qb[:, sl] * sbq).astype(BF16)
        k_ref[:, sl] = (kn[:, sl] + kpe).astype(BF16)
    v_ref[...] = _dot(ckv, wuv_ref[...]).astype(BF16)


def _proj_call(x, g_attn, win_r, g_q, wqa, wqb, g_kv, wuk, wuv, tab, pos_blocks):
    n, d = x.shape
    tm = TOK_TILE
    (n0, pb0), (_, pb1) = pos_blocks

    def tab_map(i):
        return (jnp.where(i < n0, i % pb0, (i - n0) % pb1), 0)

    row = lambda i: (i, 0)
    fix = lambda i: (0, 0)
    out_w = (MLA_HEADS * LANE, MLA_HEADS * LANE, MLA_WIDTH, NA_WIDTH, NA_WIDTH, NA_WIDTH)
    return pl.pallas_call(
        _proj_kernel,
        grid=(n // tm,),
        in_specs=[
            pl.BlockSpec((tm, d), row),
            pl.BlockSpec((1, d), fix),
            pl.BlockSpec(win_r.shape, fix),
            pl.BlockSpec((1, Q_LORA), fix),
            pl.BlockSpec(wqa.shape, fix),
            pl.BlockSpec(wqb.shape, fix),
            pl.BlockSpec((1, KV_LORA), fix),
            pl.BlockSpec(wuk.shape, fix),
            pl.BlockSpec(wuv.shape, fix),
            pl.BlockSpec((tm, 4 * LANE), tab_map),
        ],
        out_specs=[pl.BlockSpec((tm, w), row) for w in out_w],
        out_shape=[jax.ShapeDtypeStruct((n, w), BF16) for w in out_w],
        compiler_params=_cparams(("parallel",)),
        name="proj",
    )(x, g_attn, win_r, g_q, wqa, wqb, g_kv, wuk, wuv, tab)


def _mla_kernel(q_ref, k_ref, v_ref, prev_ref, o_ref, *, tk):
    del prev_ref
    tq = q_ref.shape[1]
    nchunk = k_ref.shape[1] // tk
    qs = (q_ref[0, :, 0:LANE], q_ref[0, :, LANE:2 * LANE])

    def body(c, carry):
        off = pl.multiple_of(c * tk, tk)
        vc = v_ref[0, pl.ds(off, tk), :]
        new = []
        for hh in range(2):
            m, l, acc = carry[hh]
            kc = k_ref[0, pl.ds(off, tk), hh * LANE:(hh + 1) * LANE]
            s = _dot_nt(qs[hh], kc)
            mn = jnp.maximum(m, jnp.max(s, axis=-1, keepdims=True))
            alpha = jnp.exp(m - mn)
            p = jnp.exp(s - mn)
            l = alpha * l + jnp.sum(p, axis=-1, keepdims=True)
            acc = alpha * acc + _dot(p.astype(BF16), vc)
            new.append((mn, l, acc))
        return tuple(new)

    init = tuple((jnp.full((tq, 1), NEG, F32), jnp.zeros((tq, 1), F32), jnp.zeros((tq, LANE), F32))
                 for _ in range(2))
    (_, l0, a0), (_, l1, a1) = lax.fori_loop(0, nchunk, body, init)
    lane = lax.broadcasted_iota(jnp.int32, (tq, LANE), 1)
    o_ref[0] = jnp.where(lane < V_HEAD, a0 / l0, a1 / l1).astype(o_ref.dtype)


def _mla_call(q, k, v, nb, t, boff, prev):
    tq = min(MLA_TQ, t)
    tk = min(MLA_TK, t)
    return pl.pallas_call(
        functools.partial(_mla_kernel, tk=tk),
        grid=(nb, MLA_HEADS // 2, t // tq),
        in_specs=[
            pl.BlockSpec((1, tq, 2 * LANE), lambda b, p, i: (b + boff, i, p)),
            pl.BlockSpec((1, t, 2 * LANE), lambda b, p, i: (b + boff, 0, p)),
            pl.BlockSpec((1, t, LANE), lambda b, p, i: (b + boff, 0, p)),
            pl.BlockSpec(memory_space=pl.ANY),
        ],
        out_specs=pl.BlockSpec((1, tq, LANE), lambda b, p, i: (b + boff, i, p)),
        out_shape=jax.ShapeDtypeStruct(prev.shape, prev.dtype),
        input_output_aliases={3: 0},
        compiler_params=_cparams(("parallel", "parallel", "arbitrary")),
        name="mla",
    )(q, k, v, prev)


def _na_base(i, rows):
    return jnp.clip(i * NA_RQ - NA_KH // 2, 0, rows - NA_WROWS)


def _na_kernel(q_ref, k_ref, v_ref, b_ref, prev_ref, o_ref):
    del prev_ref
    nq = q_ref.shape[1]
    rows = k_ref.shape[1] // GRID_W
    nk = NA_WROWS * GRID_W
    off = pl.multiple_of(_na_base(pl.program_id(1), rows) * GRID_W, GRID_W)
    lane = lax.broadcasted_iota(jnp.int32, (nq, LANE), 1)
    low = lane < NA_HEAD_DIM
    for p in range(NA_HEADS // 2):
        sl = slice(p * LANE, (p + 1) * LANE)
        q2 = q_ref[0, :, sl]
        kw = k_ref[0, pl.ds(off, nk), sl]
        vw = v_ref[0, pl.ds(off, nk), sl]
        outs = []
        for hh in range(2):
            qm = jnp.where(low, q2, 0) if hh == 0 else jnp.where(low, 0, q2)
            s = _dot_nt(qm.astype(BF16), kw) + b_ref[0, 2 * p + hh]
            m = jnp.max(s, axis=-1, keepdims=True)
            e = jnp.exp(s - m)
            l = jnp.sum(e, axis=-1, keepdims=True)
            outs.append(_dot(e.astype(BF16), vw) / l)
        o_ref[0, :, sl] = jnp.where(low, outs[0], outs[1]).astype(o_ref.dtype)


def _na_call(q, k, v, bias, nb, t, boff, prev):
    rows = t // GRID_W
    nq = NA_RQ * GRID_W
    nk = NA_WROWS * GRID_W

    def bias_map(b, i):
        return ((i * NA_RQ - _na_base(i, rows)) // (NA_KH // 2), 0, 0, 0)

    return pl.pallas_call(
        _na_kernel,
        grid=(nb, rows // NA_RQ),
        in_specs=[
            pl.BlockSpec((1, nq, NA_WIDTH), lambda b, i: (b + boff, i, 0)),
            pl.BlockSpec((1, t, NA_WIDTH), lambda b, i: (b + boff, 0, 0)),
            pl.BlockSpec((1, t, NA_WIDTH), lambda b, i: (b + boff, 0, 0)),
            pl.BlockSpec((1, NA_HEADS, nq, nk), bias_map),
            pl.BlockSpec(memory_space=pl.ANY),
        ],
        out_specs=pl.BlockSpec((1, nq, NA_WIDTH), lambda b, i: (b + boff, i, 0)),
        out_shape=jax.ShapeDtypeStruct(prev.shape, prev.dtype),
        input_output_aliases={4: 0},
        compiler_params=_cparams(("parallel", "arbitrary")),
        name="natten",
    )(q, k, v, bias, prev)


def _na_bias_table(rpb):
    rows = 32
    c = np.arange(GRID_W)
    cstart = np.clip(c - NA_KW // 2, 0, GRID_W - NA_KW)
    col_ok = (c[None, :] >= cstart[:, None]) & (c[None, :] < cstart[:, None] + NA_KW)
    cidx = np.clip(c[None, :] - c[:, None] + NA_KW - 1, 0, 2 * NA_KW - 2)
    onehot = (cidx.reshape(-1)[None, :] == np.arange(2 * NA_KW - 1)[:, None]).astype(np.float32)
    tabs = []
    for i0 in (0, NA_RQ, rows - NA_RQ):
        base = int(np.clip(i0 - NA_KH // 2, 0, rows - NA_WROWS))
        qi = i0 + np.arange(NA_RQ)
        start = np.clip(qi - NA_KH // 2, 0, rows - NA_KH)
        kr = base + np.arange(NA_WROWS)
        row_ok = (kr[None, :] >= start[:, None]) & (kr[None, :] < start[:, None] + NA_KH)
        ridx = np.clip(kr[None, :] - qi[:, None] + NA_KH - 1, 0, 2 * NA_KH - 2)
        by_row = rpb[:, ridx, :].astype(F32)
        vals = jnp.einsum("hjrx,xq->hjrq", by_row, onehot, precision=lax.Precision.HIGHEST)
        vals = vals.reshape(NA_HEADS, NA_RQ, NA_WROWS, GRID_W, GRID_W).transpose(0, 1, 3, 2, 4)
        ok = row_ok[:, None, :, None] & col_ok[None, :, None, :]
        tab = jnp.where(ok[None], vals, NEG)
        tabs.append(tab.reshape(NA_HEADS, NA_RQ * GRID_W, NA_WROWS * GRID_W))
    return jnp.stack(tabs)


def _mix_kernel(a_ref, n_ref, x_ref, gm_ref, gn_ref, wo_ref, gf_ref, wrh_ref, wrl_ref, br_ref,
                x2_ref, hf_ref, lg_ref):
    an = _rms(a_ref[...].astype(F32)) * gm_ref[...]
    nn = _rms(n_ref[...].astype(F32)) * gn_ref[...]
    cat = jnp.concatenate([an, nn], axis=-1).astype(BF16)
    x2 = x_ref[...] + _dot(cat, wo_ref[...])
    x2_ref[...] = x2
    hf = _rms(x2) * gf_ref[...]
    tm = hf.shape[0]
    for j in range(hf.shape[1] // LANE):
        hf_ref[pl.ds(j, tm, stride=SUB), :] = hf[:, j * LANE:(j + 1) * LANE]
    hi = hf.astype(BF16)
    lo = (hf - hi.astype(F32)).astype(BF16)
    lg_ref[...] = (_dot(hi, wrh_ref[...]) + (_dot(lo, wrh_ref[...]) + _dot(hi, wrl_ref[...]))) + br_ref[...]


def _mix_call(a, nat, x, g_mla, g_na, w_out, g_ffn, wr_hi, wr_lo, b_r):
    n, d = x.shape
    tm = TOK_TILE
    row = lambda i: (i, 0)
    fix = lambda i: (0, 0)
    return pl.pallas_call(
        _mix_kernel,
        grid=(n // tm,),
        in_specs=[
            pl.BlockSpec((tm, MLA_WIDTH), row),
            pl.BlockSpec((tm, NA_WIDTH), row),
            pl.BlockSpec((tm, d), row),
            pl.BlockSpec((1, MLA_WIDTH), fix),
            pl.BlockSpec((1, NA_WIDTH), fix),
            pl.BlockSpec(w_out.shape, fix),
            pl.BlockSpec((1, d), fix),
            pl.BlockSpec(wr_hi.shape, fix),
            pl.BlockSpec(wr_lo.shape, fix),
            pl.BlockSpec((1, LANE), fix),
        ],
        out_specs=[pl.BlockSpec((tm, d), row), pl.BlockSpec((tm * SUB, LANE), row), pl.BlockSpec((tm, LANE), row)],
        out_shape=[jax.ShapeDtypeStruct((n, d), F32), jax.ShapeDtypeStruct((n * SUB, LANE), F32),
                   jax.ShapeDtypeStruct((n, LANE), F32)],
        compiler_params=_cparams(("parallel",)),
        name="mix",
    )(a, nat, x, g_mla, g_na, w_out, g_ffn, wr_hi, wr_lo, b_r)


def _moe_kernel(be_ref, nu_ref, src_ref, srcn_ref, dst_ref, gt_ref, wg_ref, bg_ref, wu_ref, bu_ref,
                wd_ref, bd_ref, hf_hbm, ys_hbm, xbuf, ybuf, gsem, ssem):
    del be_ref
    i = pl.program_id(0)
    nu = nu_ref[0]
    nblk = pl.num_programs(0) - 2
    bm = xbuf.shape[1] // SUB
    nchunk = wd_ref.shape[2] // LANE
    blk = pl.ds(0, bm * SUB)

    def gather_tiles(idx_ref, s):
        for r in range(bm):
            src = hf_hbm.at[pl.ds(pl.multiple_of(idx_ref[0, 0, r], SUB), SUB)]
            pltpu.make_async_copy(src, xbuf.at[s, pl.ds(r * SUB, SUB)], gsem.at[s]).start(priority=r % 2)

    def scatter_tiles(s):
        for r in range(bm):
            dst = ys_hbm.at[pl.ds(pl.multiple_of(dst_ref[0, 0, r], SUB), SUB)]
            pltpu.make_async_copy(ybuf.at[s, pl.ds(r * SUB, SUB)], dst, ssem.at[s]).start(priority=r % 2)

    def step(s):
        @pl.when(i + 1 < nu)
        def _():
            gather_tiles(srcn_ref, 1 - s)

        @pl.when(i >= 2)
        def _():
            pltpu.make_async_copy(ybuf.at[s], ys_hbm.at[blk], ssem.at[s]).wait()

        @pl.when((i >= nu) & (i < nblk))
        def _():
            ybuf[s] = jnp.zeros(ybuf.shape[1:], F32)
            rows = pl.ds(pl.multiple_of(i * (bm * SUB), SUB), bm * SUB)
            pltpu.make_async_copy(ybuf.at[s], ys_hbm.at[rows], ssem.at[s]).start()

        @pl.when(i < nu)
        def _():
            pltpu.make_async_copy(hf_hbm.at[blk], xbuf.at[s], gsem.at[s]).wait()
            x = jnp.concatenate([xbuf[s, pl.ds(j, bm, stride=SUB), :] for j in range(nchunk)], axis=1).astype(BF16)
            g = _dot(x, wg_ref[0]) + bg_ref[0]
            u = _dot(x, wu_ref[0]) + bu_ref[0]
            g = jnp.minimum(g, SWIGLU_LIMIT)
            u = jnp.clip(u, -SWIGLU_LIMIT, SWIGLU_LIMIT)
            h = (u + 1.0) * (g * jax.nn.sigmoid(SWIGLU_ALPHA * g))
            y = _dot(h.astype(BF16), wd_ref[0]) + bd_ref[0]
            gate = gt_ref[...]
            for j in range(nchunk):
                ybuf[s, pl.ds(j, bm, stride=SUB), :] = y[:, j * LANE:(j + 1) * LANE] * gate
            scatter_tiles(s)

    @pl.when(i == 0)
    def _():
        gather_tiles(src_ref, 0)

    for s in range(2):
        pl.when(lax.rem(i, 2) == s)(functools.partial(step, s))


def _moe_call(blk_e, nused, src, dst, gates, wg, bg, wu, bu, wd, bd, hf):
    nblk, _, bm = src.shape
    d = wg.shape[1]
    dff = wg.shape[2]
    assert d == SUB * LANE and hf.shape[1] == LANE
    last = lambda i, nu: jnp.minimum(i, nu[0] - 1)
    cur3 = lambda i, be, nu: (last(i, nu), 0, 0)
    nxt3 = lambda i, be, nu: (last(i + 1, nu), 0, 0)
    wmap = lambda i, be, nu: (be[last(i, nu)], 0, 0)
    smem = functools.partial(pl.BlockSpec, (1, 1, bm), memory_space=pltpu.SMEM)
    grid_spec = pltpu.PrefetchScalarGridSpec(
        num_scalar_prefetch=2,
        grid=(nblk + 2,),
        in_specs=[
            smem(cur3), smem(nxt3), smem(cur3),
            pl.BlockSpec((bm, LANE), lambda i, be, nu: (last(i, nu), 0)),
            pl.BlockSpec((1, d, dff), wmap),
            pl.BlockSpec((1, 1, dff), wmap),
            pl.BlockSpec((1, d, dff), wmap),
            pl.BlockSpec((1, 1, dff), wmap),
            pl.BlockSpec((1, dff, d), wmap),
            pl.BlockSpec((1, 1, d), wmap),
            pl.BlockSpec(memory_space=pl.ANY),
        ],
        out_specs=pl.BlockSpec(memory_space=pl.ANY),
        scratch_shapes=[
            pltpu.VMEM((2, bm * SUB, LANE), F32),
            pltpu.VMEM((2, bm * SUB, LANE), F32),
            pltpu.SemaphoreType.DMA((2,)),
            pltpu.SemaphoreType.DMA((2,)),
        ],
    )
    return pl.pallas_call(
        _moe_kernel,
        grid_spec=grid_spec,
        out_shape=jax.ShapeDtypeStruct((nblk * bm * SUB, LANE), F32),
        compiler_params=_cparams(("arbitrary",)),
        name="moe",
    )(blk_e, nused, src, src, dst, gates, wg, bg, wu, bu, wd, bd, hf)


def _final_kernel(x_ref, *rest):
    y_refs, g_ref, o_ref = rest[:TOP_K], rest[TOP_K], rest[TOP_K + 1]
    tm = x_ref.shape[0]
    chunks = []
    for j in range(x_ref.shape[1] // LANE):
        parts = [y[pl.ds(j, tm, stride=SUB), :] for y in y_refs]
        chunks.append((parts[0] + parts[1]) + (parts[2] + parts[3]))
    o_ref[...] = _rms(x_ref[...] + jnp.concatenate(chunks, axis=1)) * g_ref[...]


def _final_call(x2, ys, g_final, tok0, ntok):
    n, d = x2.shape
    assert TOP_K == 4 and d == SUB * LANE
    tm = TOK_TILE // 2
    assert tok0 % tm == 0 and ntok % tm == 0 and n % tm == 0
    off = tok0 // tm
    ymap = lambda k: (lambda i: (k * (n // tm) + off + i, 0))
    return pl.pallas_call(
        _final_kernel,
        grid=(ntok // tm,),
        in_specs=[pl.BlockSpec((tm, d), lambda i: (i + off, 0))]
        + [pl.BlockSpec((tm * SUB, LANE), ymap(k)) for k in range(TOP_K)]
        + [pl.BlockSpec((1, d), lambda i: (0, 0))],
        out_specs=pl.BlockSpec((tm, d), lambda i: (i, 0)),
        out_shape=jax.ShapeDtypeStruct((ntok, d), F32),
        compiler_params=_cparams(("parallel",)),
        name="final",
    )(x2, ys, ys, ys, ys, g_final)


def _prep_weights(w_in, w_uq, w_ukv, t_max):
    d = w_in.shape[0]
    o = np.cumsum((0, Q_LORA, KV_LORA, ROPE_DIM, NA_WIDTH, NA_WIDTH, NA_WIDTH))
    w_cq, w_ckv, w_kpe = w_in[:, o[0]:o[1]], w_in[:, o[1]:o[2]], w_in[:, o[2]:o[3]]
    w_naq, w_nak, w_nav = w_in[:, o[3]:o[4]], w_in[:, o[4]:o[5]], w_in[:, o[5]:o[6]]
    half = ROPE_DIM // 2
    swap = np.concatenate([np.arange(half, ROPE_DIM), np.arange(half)])
    zpad = lambda w, lo: jnp.pad(w, ((0, 0), (lo, LANE - lo - w.shape[1])))
    win_r = jnp.concatenate([w_naq, w_nak, w_nav, w_cq, w_ckv,
                             zpad(w_kpe, QK_NOPE), zpad(w_kpe[:, swap], QK_NOPE)], axis=1).astype(BF16)

    uq = w_uq.reshape(Q_LORA, MLA_HEADS, QK_HEAD)
    wqa = jnp.pad(uq, ((0, 0), (0, 0), (0, LANE - QK_HEAD))).reshape(Q_LORA, MLA_HEADS * LANE).astype(BF16)
    wqb = jnp.pad(uq[:, :, QK_NOPE:][:, :, swap], ((0, 0), (0, 0), (QK_NOPE, LANE - QK_HEAD)))
    wqb = wqb.reshape(Q_LORA, MLA_HEADS * LANE).astype(BF16)
    ukv = w_ukv.reshape(KV_LORA, MLA_HEADS, QK_NOPE + V_HEAD)
    wuk = jnp.pad(ukv[:, :, :QK_NOPE], ((0, 0), (0, 0), (0, LANE - QK_NOPE)))
    wuk = wuk.reshape(KV_LORA, MLA_HEADS * LANE).astype(BF16)
    wuv = ukv[:, :, QK_NOPE:].reshape(KV_LORA, MLA_WIDTH).astype(BF16)

    pos = jnp.arange(t_max, dtype=F32)
    inv = 1.0 / (ROPE_THETA ** (jnp.arange(0, ROPE_DIM, 2, dtype=F32) / ROPE_DIM))
    ang = pos[:, None] * inv[None, :]
    cos, sin = jnp.cos(ang), jnp.sin(ang)
    c32 = jnp.concatenate([cos, cos], axis=1)
    s32 = jnp.concatenate([-sin, sin], axis=1)
    z = lambda w: jnp.zeros((t_max, w), F32)
    cak = jnp.concatenate([z(QK_NOPE), c32, z(LANE - QK_HEAD)], axis=1)
    sbk = jnp.concatenate([z(QK_NOPE), s32, z(LANE - QK_HEAD)], axis=1)
    scale = QK_HEAD ** -0.5
    caq = scale * jnp.concatenate([jnp.ones((t_max, QK_NOPE), F32), c32, z(LANE - QK_HEAD)], axis=1)
    tab = jnp.concatenate([caq, scale * sbk, cak, sbk], axis=1)
    del d
    return win_r, wqa, wqb, wuk, wuv, tab


def _route(logits, bm):
    n = logits.shape[0]
    a = n * TOP_K
    nblk = a // bm + N_EXPERTS
    top_v, top_i = lax.top_k(logits, TOP_K)
    gates = jax.nn.softmax(top_v, axis=-1)
    flat_e = top_i.reshape(-1).astype(jnp.int32)
    order = jnp.argsort(flat_e, stable=True).astype(jnp.int32)
    experts = jnp.arange(N_EXPERTS, dtype=jnp.int32)
    counts = jnp.sum((flat_e[None, :] == experts[:, None]).astype(jnp.int32), axis=1)
    start = jnp.cumsum(counts) - counts
    padded = ((counts + bm - 1) // bm) * bm
    pad_end = jnp.cumsum(padded)
    pad_start = pad_end - padded
    blk0 = jnp.arange(nblk, dtype=jnp.int32) * bm
    blk_e = jnp.minimum(jnp.sum((pad_end[None, :] <= blk0[:, None]).astype(jnp.int32), axis=1), N_EXPERTS - 1)
    nused = (pad_end[-1:] // bm).astype(jnp.int32)
    rank = (blk0 - pad_start[blk_e])[:, None] + jnp.arange(bm, dtype=jnp.int32)[None, :]
    cnt = counts[blk_e][:, None]
    valid = rank < cnt
    aid = order[jnp.clip(start[blk_e][:, None] + rank, 0, a - 1)]
    src = jnp.where(valid, aid // TOP_K, 0)
    pad_rank = (pad_start - start)[blk_e][:, None] + rank - cnt
    dst = jnp.where(valid, (aid % TOP_K) * n + aid // TOP_K, a + pad_rank)
    gate = jnp.where(valid, gates.reshape(-1)[aid], 0.0)
    return (src * SUB)[:, None, :], (dst * SUB)[:, None, :], gate.reshape(-1), blk_e.astype(jnp.int32), nused


def kernel(x_prompt, x_sample, g_attn_norm, w_in, g_q_lora, w_uq, g_kv_lora, w_ukv, na_rpb, g_out_mla,
           g_out_na, w_out, g_ffn_norm, w_router, b_router, w_gate, b_gate, w_up, b_up, w_down, b_down,
           g_final):
    assert g_attn_norm.shape[0] == 1, "single-layer trunk"
    d = x_prompt.shape[-1]
    groups = [x_prompt, x_sample]
    shapes = [(g.shape[0], g.shape[1]) for g in groups]
    for _, t in shapes:
        assert t % TOK_TILE == 0 and t % GRID_W == 0 and (t // GRID_W) % NA_RQ == 0
        assert t // GRID_W >= NA_WROWS + NA_RQ
    x = jnp.concatenate([g.reshape(-1, d) for g in groups], axis=0)
    n = x.shape[0]
    t_max = max(t for _, t in shapes)

    win_r, wqa, wqb, wuk, wuv, tab = _prep_weights(w_in[0], w_uq[0], w_ukv[0], t_max)
    pos_blocks = [(b * t // TOK_TILE, t // TOK_TILE) for b, t in shapes]
    row = lambda v: v.reshape(1, -1)
    q, k, v, naq, nak, nav = _proj_call(x, row(g_attn_norm[0]), win_r, row(g_q_lora[0]), wqa, wqb,
                                        row(g_kv_lora[0]), wuk, wuv, tab, pos_blocks)

    bias = _na_bias_table(na_rpb[0])
    a_all = jnp.zeros((n, MLA_WIDTH), BF16)
    n_all = jnp.zeros((n, NA_WIDTH), BF16)
    tok0 = 0
    for b, t in shapes:
        assert tok0 % t == 0 and n % t == 0
        boff = tok0 // t
        view = lambda arr: arr.reshape(n // t, t, arr.shape[-1])
        a_all = _mla_call(view(q), view(k), view(v), b, t, boff, view(a_all)).reshape(n, MLA_WIDTH)
        n_all = _na_call(view(naq), view(nak), view(nav), bias, b, t, boff, view(n_all)).reshape(n, NA_WIDTH)
        tok0 += b * t

    wr = jnp.pad(w_router[0], ((0, 0), (0, LANE - N_EXPERTS)))
    wr_hi = wr.astype(BF16)
    wr_lo = (wr - wr_hi.astype(F32)).astype(BF16)
    b_r = jnp.pad(b_router[0], (0, LANE - N_EXPERTS)).reshape(1, LANE)
    x2, hf, logits = _mix_call(a_all, n_all, x, row(g_out_mla[0]), row(g_out_na[0]), w_out[0].astype(BF16),
                               row(g_ffn_norm[0]), wr_hi, wr_lo, b_r)

    src, dst, gate, blk_e, nused = _route(logits[:, :N_EXPERTS], MOE_BM)
    gates_b = jnp.broadcast_to(gate[:, None], (gate.shape[0], LANE))
    ys = _moe_call(blk_e, nused, src, dst, gates_b,
                   w_gate[0].astype(BF16), b_gate[0][:, None, :], w_up[0].astype(BF16), b_up[0][:, None, :],
                   w_down[0].astype(BF16), b_down[0][:, None, :], hf)

    outs, tok0 = [], 0
    for b, t in shapes:
        outs.append(_final_call(x2, ys, row(g_final), tok0, b * t).reshape(b, t, d))
        tok0 += b * t
    return tuple(outs)
```

```python
import functools

import numpy as np
import jax
import jax.numpy as jnp
from jax import lax
from jax.experimental import pallas as pl
from jax.experimental.pallas import tpu as pltpu

F32 = jnp.float32
BF16 = jnp.bfloat16

EPS = 1e-6
MLA_HEADS = 8
Q_LORA = 256
KV_LORA = 128
QK_NOPE = 64
ROPE_DIM = 32
QK_HEAD = QK_NOPE + ROPE_DIM
V_HEAD = 64
ROPE_THETA = 10000.0
NA_HEADS = 8
NA_HEAD_DIM = 64
NA_WIDTH = NA_HEADS * NA_HEAD_DIM
MLA_WIDTH = MLA_HEADS * V_HEAD
GRID_W = 64
NA_KH = 8
NA_KW = 16
N_EXPERTS = 32
TOP_K = 4
SWIGLU_LIMIT = 7.0
SWIGLU_ALPHA = 1.702

LANE = 128
SUB = 8
VMEM_LIMIT = 56 * 1024 * 1024

TOK_TILE = 512
MLA_TQ = 256
LOG2E = 1.4426950408889634
NA_RQ = 4
NA_WROWS = 12
MOE_BM = 512
NEG = -1e30

IN_COLS = 3 * NA_WIDTH + Q_LORA + KV_LORA + 2 * LANE


def _rms(x):
    return x * lax.rsqrt(jnp.mean(x * x, axis=-1, keepdims=True) + EPS)


def _dot(a, b):
    return jnp.dot(a, b, preferred_element_type=F32)


def _dot_nt(a, b):
    return lax.dot_general(a, b, (((1,), (1,)), ((), ())), preferred_element_type=F32)


def _cparams(sem):
    return pltpu.CompilerParams(dimension_semantics=sem, vmem_limit_bytes=VMEM_LIMIT)


def _proj_kernel(x_ref, g_ref, win_ref, gq_ref, wqa_ref, wqb_ref, gkv_ref, wuk_ref, wuv_ref, tab_ref,
                 q_ref, k_ref, v_ref, naq_ref, nak_ref, nav_ref):
    h = (_rms(x_ref[...]) * g_ref[...]).astype(BF16)
    proj = _dot(h, win_ref[...])
    o = 0
    naq_ref[...] = (proj[:, o:o + NA_WIDTH] * (NA_HEAD_DIM ** -0.5 * LOG2E)).astype(BF16)
    o += NA_WIDTH
    nak_ref[...] = proj[:, o:o + NA_WIDTH].astype(BF16)
    o += NA_WIDTH
    nav_ref[...] = proj[:, o:o + NA_WIDTH].astype(BF16)
    o += NA_WIDTH
    cq = (_rms(proj[:, o:o + Q_LORA]) * gq_ref[...]).astype(BF16)
    o += Q_LORA
    ckv = (_rms(proj[:, o:o + KV_LORA]) * gkv_ref[...]).astype(BF16)
    o += KV_LORA
    kpe_a = proj[:, o:o + LANE]
    kpe_b = proj[:, o + LANE:o + 2 * LANE]

    tab = tab_ref[...]
    caq, sbq = tab[:, 0:LANE], tab[:, LANE:2 * LANE]
    cak, sbk = tab[:, 2 * LANE:3 * LANE], tab[:, 3 * LANE:4 * LANE]

    qa = _dot(cq, wqa_ref[...])
    qb = _dot(cq, wqb_ref[...])
    kn = _dot(ckv, wuk_ref[...])
    kpe = kpe_a * cak + kpe_b * sbk
    for hd in range(MLA_HEADS):
        sl = slice(hd * LANE, (hd + 1) * LANE)
        q_ref[:, sl] = (qa[:, sl] * caq + qb[:, sl] * sbq).astype(BF16)
        k_ref[:, sl] = (kn[:, sl] + kpe).astype(BF16)
    v_ref[...] = _dot(ckv, wuv_ref[...]).astype(BF16)


def _proj_call(x, g_attn, win_r, g_q, wqa, wqb, g_kv, wuk, wuv, tab, pos_blocks):
    n, d = x.shape
    tm = TOK_TILE
    (n0, pb0), (_, pb1) = pos_blocks

    def tab_map(i):
        return (jnp.where(i < n0, i % pb0, (i - n0) % pb1), 0)

    row = lambda i: (i, 0)
    fix = lambda i: (0, 0)
    out_w = (MLA_HEADS * LANE, MLA_HEADS * LANE, MLA_WIDTH, NA_WIDTH, NA_WIDTH, NA_WIDTH)
    return pl.pallas_call(
        _proj_kernel,
        grid=(n // tm,),
        in_specs=[
            pl.BlockSpec((tm, d), row),
            pl.BlockSpec((1, d), fix),
            pl.BlockSpec(win_r.shape, fix),
            pl.BlockSpec((1, Q_LORA), fix),
            pl.BlockSpec(wqa.shape, fix),
            pl.BlockSpec(wqb.shape, fix),
            pl.BlockSpec((1, KV_LORA), fix),
            pl.BlockSpec(wuk.shape, fix),
            pl.BlockSpec(wuv.shape, fix),
            pl.BlockSpec((tm, 4 * LANE), tab_map),
        ],
        out_specs=[pl.BlockSpec((tm, w), row) for w in out_w],
        out_shape=[jax.ShapeDtypeStruct((n, w), BF16) for w in out_w],
        compiler_params=_cparams(("parallel",)),
        name="proj",
    )(x, g_attn, win_r, g_q, wqa, wqb, g_kv, wuk, wuv, tab)


def _softmax_pv(s, v_ones):
    p = jnp.exp2(s - jnp.max(s, axis=-1, keepdims=True))
    acc = _dot(p.astype(BF16), v_ones)
    return acc[:, :LANE] / acc[:, LANE:]


def _mla_kernel(q_ref, k_ref, v_ref, prev_ref, o_ref):
    del prev_ref
    tq, t = q_ref.shape[1], k_ref.shape[1]
    low = lax.broadcasted_iota(jnp.int32, (tq, LANE), 1) < V_HEAD
    ones = jnp.ones((t, LANE), BF16)
    for pp in range(MLA_HEADS // 2):
        v_ones = jnp.concatenate([v_ref[0, :, pp * LANE:(pp + 1) * LANE], ones], axis=1)
        res = []
        for h in (2 * pp, 2 * pp + 1):
            s = _dot_nt(q_ref[0, :, h * LANE:(h + 1) * LANE], k_ref[0, :, h * LANE:(h + 1) * LANE])
            res.append(_softmax_pv(s, v_ones))
        o_ref[0, :, pp * LANE:(pp + 1) * LANE] = jnp.where(low, res[0], res[1]).astype(o_ref.dtype)


def _mla_call(q, k, v, nb, t, boff, prev):
    tq = min(MLA_TQ, t)
    return pl.pallas_call(
        _mla_kernel,
        grid=(nb, t // tq),
        in_specs=[
            pl.BlockSpec((1, tq, MLA_HEADS * LANE), lambda b, i: (b + boff, i, 0)),
            pl.BlockSpec((1, t, MLA_HEADS * LANE), lambda b, i: (b + boff, 0, 0)),
            pl.BlockSpec((1, t, MLA_WIDTH), lambda b, i: (b + boff, 0, 0)),
            pl.BlockSpec(memory_space=pl.ANY),
        ],
        out_specs=pl.BlockSpec((1, tq, MLA_WIDTH), lambda b, i: (b + boff, i, 0)),
        out_shape=jax.ShapeDtypeStruct(prev.shape, prev.dtype),
        input_output_aliases={3: 0},
        compiler_params=_cparams(("parallel", "arbitrary")),
        name="mla",
    )(q, k, v, prev)


def _na_base(i, rows):
    return jnp.clip(i * NA_RQ - NA_KH // 2, 0, rows - NA_WROWS)


def _na_kernel(q_ref, k_ref, v_ref, b_ref, prev_ref, o_ref):
    del prev_ref
    nq = q_ref.shape[1]
    rows = k_ref.shape[1] // GRID_W
    nk = NA_WROWS * GRID_W
    off = pl.multiple_of(_na_base(pl.program_id(1), rows) * GRID_W, GRID_W)
    lane = lax.broadcasted_iota(jnp.int32, (nq, LANE), 1)
    low = lane < NA_HEAD_DIM
    ones = jnp.ones((nk, LANE), BF16)
    for p in range(NA_HEADS // 2):
        sl = slice(p * LANE, (p + 1) * LANE)
        q2 = q_ref[0, :, sl]
        kw = k_ref[0, pl.ds(off, nk), sl]
        v_ones = jnp.concatenate([v_ref[0, pl.ds(off, nk), sl], ones], axis=1)
        outs = []
        for hh in range(2):
            qm = jnp.where(low, q2, 0) if hh == 0 else jnp.where(low, 0, q2)
            s = _dot_nt(qm, kw) + b_ref[0, 2 * p + hh]
            outs.append(_softmax_pv(s, v_ones))
        o_ref[0, :, sl] = jnp.where(low, outs[0], outs[1]).astype(o_ref.dtype)


def _na_call(q, k, v, bias, nb, t, boff, prev):
    rows = t // GRID_W
    nq = NA_RQ * GRID_W
    nk = NA_WROWS * GRID_W

    def bias_map(b, i):
        return ((i * NA_RQ - _na_base(i, rows)) // (NA_KH // 2), 0, 0, 0)

    return pl.pallas_call(
        _na_kernel,
        grid=(nb, rows // NA_RQ),
        in_specs=[
            pl.BlockSpec((1, nq, NA_WIDTH), lambda b, i: (b + boff, i, 0)),
            pl.BlockSpec((1, t, NA_WIDTH), lambda b, i: (b + boff, 0, 0)),
            pl.BlockSpec((1, t, NA_WIDTH), lambda b, i: (b + boff, 0, 0)),
            pl.BlockSpec((1, NA_HEADS, nq, nk), bias_map),
            pl.BlockSpec(memory_space=pl.ANY),
        ],
        out_specs=pl.BlockSpec((1, nq, NA_WIDTH), lambda b, i: (b + boff, i, 0)),
        out_shape=jax.ShapeDtypeStruct(prev.shape, prev.dtype),
        input_output_aliases={4: 0},
        compiler_params=_cparams(("parallel", "arbitrary")),
        name="natten",
    )(q, k, v, bias, prev)


def _na_bias_table(rpb):
    rows = 32
    c = np.arange(GRID_W)
    cstart = np.clip(c - NA_KW // 2, 0, GRID_W - NA_KW)
    col_ok = (c[None, :] >= cstart[:, None]) & (c[None, :] < cstart[:, None] + NA_KW)
    cidx = np.clip(c[None, :] - c[:, None] + NA_KW - 1, 0, 2 * NA_KW - 2)
    onehot = (cidx.reshape(-1)[None, :] == np.arange(2 * NA_KW - 1)[:, None]).astype(np.float32)
    tabs = []
    for i0 in (0, NA_RQ, rows - NA_RQ):
        base = int(np.clip(i0 - NA_KH // 2, 0, rows - NA_WROWS))
        qi = i0 + np.arange(NA_RQ)
        start = np.clip(qi - NA_KH // 2, 0, rows - NA_KH)
        kr = base + np.arange(NA_WROWS)
        row_ok = (kr[None, :] >= start[:, None]) & (kr[None, :] < start[:, None] + NA_KH)
        ridx = np.clip(kr[None, :] - qi[:, None] + NA_KH - 1, 0, 2 * NA_KH - 2)
        by_row = rpb[:, ridx, :].astype(F32)
        vals = jnp.einsum("hjrx,xq->hjrq", by_row, onehot, precision=lax.Precision.HIGHEST)
        vals = vals.reshape(NA_HEADS, NA_RQ, NA_WROWS, GRID_W, GRID_W).transpose(0, 1, 3, 2, 4)
        ok = row_ok[:, None, :, None] & col_ok[None, :, None, :]
        tab = jnp.where(ok[None], vals * LOG2E, NEG)
        tabs.append(tab.reshape(NA_HEADS, NA_RQ * GRID_W, NA_WROWS * GRID_W))
    return jnp.stack(tabs)


def _mix_kernel(a_ref, n_ref, x_ref, gm_ref, gn_ref, wo_ref, gf_ref, wrh_ref, wrl_ref, br_ref,
                x2_ref, hf_ref, lg_ref):
    an = _rms(a_ref[...].astype(F32)) * gm_ref[...]
    nn = _rms(n_ref[...].astype(F32)) * gn_ref[...]
    cat = jnp.concatenate([an, nn], axis=-1).astype(BF16)
    x2 = x_ref[...] + _dot(cat, wo_ref[...])
    x2_ref[...] = x2
    hf = _rms(x2) * gf_ref[...]
    tm = hf.shape[0]
    for j in range(hf.shape[1] // LANE):
        hf_ref[pl.ds(j, tm, stride=SUB), :] = hf[:, j * LANE:(j + 1) * LANE]
    hi = hf.astype(BF16)
    lo = (hf - hi.astype(F32)).astype(BF16)
    lg_ref[...] = (_dot(hi, wrh_ref[...]) + (_dot(lo, wrh_ref[...]) + _dot(hi, wrl_ref[...]))) + br_ref[...]


def _mix_call(a, nat, x, g_mla, g_na, w_out, g_ffn, wr_hi, wr_lo, b_r):
    n, d = x.shape
    tm = TOK_TILE
    row = lambda i: (i, 0)
    fix = lambda i: (0, 0)
    return pl.pallas_call(
        _mix_kernel,
        grid=(n // tm,),
        in_specs=[
            pl.BlockSpec((tm, MLA_WIDTH), row),
            pl.BlockSpec((tm, NA_WIDTH), row),
            pl.BlockSpec((tm, d), row),
            pl.BlockSpec((1, MLA_WIDTH), fix),
            pl.BlockSpec((1, NA_WIDTH), fix),
            pl.BlockSpec(w_out.shape, fix),
            pl.BlockSpec((1, d), fix),
            pl.BlockSpec(wr_hi.shape, fix),
            pl.BlockSpec(wr_lo.shape, fix),
            pl.BlockSpec((1, LANE), fix),
        ],
        out_specs=[pl.BlockSpec((tm, d), row), pl.BlockSpec((tm * SUB, LANE), row), pl.BlockSpec((tm, LANE), row)],
        out_shape=[jax.ShapeDtypeStruct((n, d), F32), jax.ShapeDtypeStruct((n * SUB, LANE), F32),
                   jax.ShapeDtypeStruct((n, LANE), F32)],
        compiler_params=_cparams(("parallel",)),
        name="mix",
    )(a, nat, x, g_mla, g_na, w_out, g_ffn, wr_hi, wr_lo, b_r)


def _moe_kernel(be_ref, nu_ref, src_ref, srcn_ref, dst_ref, gt_ref, wgf_ref, bg_ref, wuf_ref, bu_ref,
                wdf_ref, bd_ref, hf_hbm, ys_hbm, xbuf, ybuf, wg_ref, wu_ref, wd_ref, gsem, ssem):
    i = pl.program_id(0)
    nu = nu_ref[0]
    nblk = pl.num_programs(0) - 2
    bm = xbuf.shape[1] // SUB
    nchunk = wd_ref.shape[1] // LANE
    blk = pl.ds(0, bm * SUB)

    def gather_tiles(idx_ref, s):
        for r in range(bm):
            src = hf_hbm.at[pl.ds(pl.multiple_of(idx_ref[0, 0, r], SUB), SUB)]
            pltpu.make_async_copy(src, xbuf.at[s, pl.ds(r * SUB, SUB)], gsem.at[s]).start(priority=r % 2)

    def scatter_tiles(s):
        for r in range(bm):
            dst = ys_hbm.at[pl.ds(pl.multiple_of(dst_ref[0, 0, r], SUB), SUB)]
            pltpu.make_async_copy(ybuf.at[s, pl.ds(r * SUB, SUB)], dst, ssem.at[s]).start(priority=r % 2)

    def step(s):
        @pl.when(i + 1 < nu)
        def _():
            gather_tiles(srcn_ref, 1 - s)

        @pl.when(i >= 2)
        def _():
            pltpu.make_async_copy(ybuf.at[s], ys_hbm.at[blk], ssem.at[s]).wait()

        @pl.when((i >= nu) & (i < nblk))
        def _():
            ybuf[s] = jnp.zeros(ybuf.shape[1:], F32)
            rows = pl.ds(pl.multiple_of(i * (bm * SUB), SUB), bm * SUB)
            pltpu.make_async_copy(ybuf.at[s], ys_hbm.at[rows], ssem.at[s]).start()

        @pl.when(i < nu)
        def _():
            pltpu.make_async_copy(hf_hbm.at[blk], xbuf.at[s], gsem.at[s]).wait()
            x = jnp.concatenate([xbuf[s, pl.ds(j, bm, stride=SUB), :] for j in range(nchunk)], axis=1).astype(BF16)
            g = _dot(x, wg_ref[...]) + bg_ref[0]
            u = _dot(x, wu_ref[...]) + bu_ref[0]
            g = jnp.minimum(g, SWIGLU_LIMIT)
            u = jnp.clip(u, -SWIGLU_LIMIT, SWIGLU_LIMIT)
            h = (u + 1.0) * (g * jax.nn.sigmoid(SWIGLU_ALPHA * g))
            y = _dot(h.astype(BF16), wd_ref[...]) + bd_ref[0]
            gate = gt_ref[...]
            for j in range(nchunk):
                ybuf[s, pl.ds(j, bm, stride=SUB), :] = y[:, j * LANE:(j + 1) * LANE] * gate
            scatter_tiles(s)

    @pl.when(i == 0)
    def _():
        gather_tiles(src_ref, 0)

    blk_i = jnp.minimum(i, nu - 1)

    @pl.when((i == 0) | ((i < nu) & (be_ref[blk_i] != be_ref[jnp.maximum(blk_i - 1, 0)])))
    def _():
        wg_ref[...] = wgf_ref[0].astype(BF16)
        wu_ref[...] = wuf_ref[0].astype(BF16)
        wd_ref[...] = wdf_ref[0].astype(BF16)

    for s in range(2):
        pl.when(lax.rem(i, 2) == s)(functools.partial(step, s))


def _moe_call(blk_e, nused, src, dst, gates, wg, bg, wu, bu, wd, bd, hf):
    nblk, _, bm = src.shape
    d = wg.shape[1]
    dff = wg.shape[2]
    assert d == SUB * LANE and hf.shape[1] == LANE
    last = lambda i, nu: jnp.minimum(i, nu[0] - 1)
    cur3 = lambda i, be, nu: (last(i, nu), 0, 0)
    nxt3 = lambda i, be, nu: (last(i + 1, nu), 0, 0)
    wmap = lambda i, be, nu: (be[last(i, nu)], 0, 0)
    smem = functools.partial(pl.BlockSpec, (1, 1, bm), memory_space=pltpu.SMEM)
    grid_spec = pltpu.PrefetchScalarGridSpec(
        num_scalar_prefetch=2,
        grid=(nblk + 2,),
        in_specs=[
            smem(cur3), smem(nxt3), smem(cur3),
            pl.BlockSpec((bm, LANE), lambda i, be, nu: (last(i, nu), 0)),
            pl.BlockSpec((1, d, dff), wmap),
            pl.BlockSpec((1, 1, dff), wmap),
            pl.BlockSpec((1, d, dff), wmap),
            pl.BlockSpec((1, 1, dff), wmap),
            pl.BlockSpec((1, dff, d), wmap),
            pl.BlockSpec((1, 1, d), wmap),
            pl.BlockSpec(memory_space=pl.ANY),
        ],
        out_specs=pl.BlockSpec(memory_space=pl.ANY),
        scratch_shapes=[
            pltpu.VMEM((2, bm * SUB, LANE), F32),
            pltpu.VMEM((2, bm * SUB, LANE), F32),
            pltpu.VMEM((d, dff), BF16),
            pltpu.VMEM((d, dff), BF16),
            pltpu.VMEM((dff, d), BF16),
            pltpu.SemaphoreType.DMA((2,)),
            pltpu.SemaphoreType.DMA((2,)),
        ],
    )
    return pl.pallas_call(
        _moe_kernel,
        grid_spec=grid_spec,
        out_shape=jax.ShapeDtypeStruct((nblk * bm * SUB, LANE), F32),
        compiler_params=_cparams(("arbitrary",)),
        name="moe",
    )(blk_e, nused, src, src, dst, gates, wg, bg, wu, bu, wd, bd, hf)


def _final_kernel(x_ref, *rest):
    y_refs, g_ref, o_ref = rest[:TOP_K], rest[TOP_K], rest[TOP_K + 1]
    tm = x_ref.shape[0]
    chunks = []
    for j in range(x_ref.shape[1] // LANE):
        parts = [y[pl.ds(j, tm, stride=SUB), :] for y in y_refs]
        chunks.append((parts[0] + parts[1]) + (parts[2] + parts[3]))
    o_ref[...] = _rms(x_ref[...] + jnp.concatenate(chunks, axis=1)) * g_ref[...]


def _final_call(x2, ys, g_final, tok0, ntok):
    n, d = x2.shape
    assert TOP_K == 4 and d == SUB * LANE
    tm = TOK_TILE // 2
    assert tok0 % tm == 0 and ntok % tm == 0 and n % tm == 0
    off = tok0 // tm
    ymap = lambda k: (lambda i: (k * (n // tm) + off + i, 0))
    return pl.pallas_call(
        _final_kernel,
        grid=(ntok // tm,),
        in_specs=[pl.BlockSpec((tm, d), lambda i: (i + off, 0))]
        + [pl.BlockSpec((tm * SUB, LANE), ymap(k)) for k in range(TOP_K)]
        + [pl.BlockSpec((1, d), lambda i: (0, 0))],
        out_specs=pl.BlockSpec((tm, d), lambda i: (i, 0)),
        out_shape=jax.ShapeDtypeStruct((ntok, d), F32),
        compiler_params=_cparams(("parallel",)),
        name="final",
    )(x2, ys, ys, ys, ys, g_final)


def _prep_weights(w_in, w_uq, w_ukv, t_max):
    d = w_in.shape[0]
    o = np.cumsum((0, Q_LORA, KV_LORA, ROPE_DIM, NA_WIDTH, NA_WIDTH, NA_WIDTH))
    w_cq, w_ckv, w_kpe = w_in[:, o[0]:o[1]], w_in[:, o[1]:o[2]], w_in[:, o[2]:o[3]]
    w_naq, w_nak, w_nav = w_in[:, o[3]:o[4]], w_in[:, o[4]:o[5]], w_in[:, o[5]:o[6]]
    half = ROPE_DIM // 2
    swap = np.concatenate([np.arange(half, ROPE_DIM), np.arange(half)])
    zpad = lambda w, lo: jnp.pad(w, ((0, 0), (lo, LANE - lo - w.shape[1])))
    win_r = jnp.concatenate([w_naq, w_nak, w_nav, w_cq, w_ckv,
                             zpad(w_kpe, QK_NOPE), zpad(w_kpe[:, swap], QK_NOPE)], axis=1).astype(BF16)

    uq = w_uq.reshape(Q_LORA, MLA_HEADS, QK_HEAD)
    wqa = jnp.pad(uq, ((0, 0), (0, 0), (0, LANE - QK_HEAD))).reshape(Q_LORA, MLA_HEADS * LANE).astype(BF16)
    wqb = jnp.pad(uq[:, :, QK_NOPE:][:, :, swap], ((0, 0), (0, 0), (QK_NOPE, LANE - QK_HEAD)))
    wqb = wqb.reshape(Q_LORA, MLA_HEADS * LANE).astype(BF16)
    ukv = w_ukv.reshape(KV_LORA, MLA_HEADS, QK_NOPE + V_HEAD)
    wuk = jnp.pad(ukv[:, :, :QK_NOPE], ((0, 0), (0, 0), (0, LANE - QK_NOPE)))
    wuk = wuk.reshape(KV_LORA, MLA_HEADS * LANE).astype(BF16)
    wuv = ukv[:, :, QK_NOPE:].reshape(KV_LORA, MLA_WIDTH).astype(BF16)

    pos = jnp.arange(t_max, dtype=F32)
    inv = 1.0 / (ROPE_THETA ** (jnp.arange(0, ROPE_DIM, 2, dtype=F32) / ROPE_DIM))
    ang = pos[:, None] * inv[None, :]
    cos, sin = jnp.cos(ang), jnp.sin(ang)
    c32 = jnp.concatenate([cos, cos], axis=1)
    s32 = jnp.concatenate([-sin, sin], axis=1)
    z = lambda w: jnp.zeros((t_max, w), F32)
    cak = jnp.concatenate([z(QK_NOPE), c32, z(LANE - QK_HEAD)], axis=1)
    sbk = jnp.concatenate([z(QK_NOPE), s32, z(LANE - QK_HEAD)], axis=1)
    scale = QK_HEAD ** -0.5 * LOG2E
    caq = scale * jnp.concatenate([jnp.ones((t_max, QK_NOPE), F32), c32, z(LANE - QK_HEAD)], axis=1)
    tab = jnp.concatenate([caq, scale * sbk, cak, sbk], axis=1)
    del d
    return win_r, wqa, wqb, wuk, wuv, tab


def _route(logits, bm):
    n = logits.shape[0]
    a = n * TOP_K
    nblk = a // bm + N_EXPERTS
    top_v, top_i = lax.top_k(logits, TOP_K)
    gates = jax.nn.softmax(top_v, axis=-1)
    flat_e = top_i.reshape(-1).astype(jnp.int32)
    order = jnp.argsort(flat_e, stable=True).astype(jnp.int32)
    experts = jnp.arange(N_EXPERTS, dtype=jnp.int32)
    counts = jnp.sum((flat_e[None, :] == experts[:, None]).astype(jnp.int32), axis=1)
    start = jnp.cumsum(counts) - counts
    padded = ((counts + bm - 1) // bm) * bm
    pad_end = jnp.cumsum(padded)
    pad_start = pad_end - padded
    blk0 = jnp.arange(nblk, dtype=jnp.int32) * bm
    blk_e = jnp.minimum(jnp.sum((pad_end[None, :] <= blk0[:, None]).astype(jnp.int32), axis=1), N_EXPERTS - 1)
    nused = (pad_end[-1:] // bm).astype(jnp.int32)
    rank = (blk0 - pad_start[blk_e])[:, None] + jnp.arange(bm, dtype=jnp.int32)[None, :]
    cnt = counts[blk_e][:, None]
    valid = rank < cnt
    aid = order[jnp.clip(start[blk_e][:, None] + rank, 0, a - 1)]
    src = jnp.where(valid, aid // TOP_K, 0)
    pad_rank = (pad_start - start)[blk_e][:, None] + rank - cnt
    dst = jnp.where(valid, (aid % TOP_K) * n + aid // TOP_K, a + pad_rank)
    gate = jnp.where(valid, gates.reshape(-1)[aid], 0.0)
    return (src * SUB)[:, None, :], (dst * SUB)[:, None, :], gate.reshape(-1), blk_e.astype(jnp.int32), nused


def kernel(x_prompt, x_sample, g_attn_norm, w_in, g_q_lora, w_uq, g_kv_lora, w_ukv, na_rpb, g_out_mla,
           g_out_na, w_out, g_ffn_norm, w_router, b_router, w_gate, b_gate, w_up, b_up, w_down, b_down,
           g_final):
    assert g_attn_norm.shape[0] == 1, "single-layer trunk"
    d = x_prompt.shape[-1]
    groups = [x_prompt, x_sample]
    shapes = [(g.shape[0], g.shape[1]) for g in groups]
    for _, t in shapes:
        assert t % TOK_TILE == 0 and t % GRID_W == 0 and (t // GRID_W) % NA_RQ == 0
        assert t // GRID_W >= NA_WROWS + NA_RQ
    x = jnp.concatenate([g.reshape(-1, d) for g in groups], axis=0)
    n = x.shape[0]
    t_max = max(t for _, t in shapes)

    win_r, wqa, wqb, wuk, wuv, tab = _prep_weights(w_in[0], w_uq[0], w_ukv[0], t_max)
    pos_blocks = [(b * t // TOK_TILE, t // TOK_TILE) for b, t in shapes]
    row = lambda v: v.reshape(1, -1)
    q, k, v, naq, nak, nav = _proj_call(x, row(g_attn_norm[0]), win_r, row(g_q_lora[0]), wqa, wqb,
                                        row(g_kv_lora[0]), wuk, wuv, tab, pos_blocks)

    bias = _na_bias_table(na_rpb[0])
    a_all = jnp.zeros((n, MLA_WIDTH), BF16)
    n_all = jnp.zeros((n, NA_WIDTH), BF16)
    tok0 = 0
    for b, t in shapes:
        assert tok0 % t == 0 and n % t == 0
        boff = tok0 // t
        view = lambda arr: arr.reshape(n // t, t, arr.shape[-1])
        a_all = _mla_call(view(q), view(k), view(v), b, t, boff, view(a_all)).reshape(n, MLA_WIDTH)
        n_all = _na_call(view(naq), view(nak), view(nav), bias, b, t, boff, view(n_all)).reshape(n, NA_WIDTH)
        tok0 += b * t

    wr = jnp.pad(w_router[0], ((0, 0), (0, LANE - N_EXPERTS)))
    wr_hi = wr.astype(BF16)
    wr_lo = (wr - wr_hi.astype(F32)).astype(BF16)
    b_r = jnp.pad(b_router[0], (0, LANE - N_EXPERTS)).reshape(1, LANE)
    x2, hf, logits = _mix_call(a_all, n_all, x, row(g_out_mla[0]), row(g_out_na[0]), w_out[0].astype(BF16),
                               row(g_ffn_norm[0]), wr_hi, wr_lo, b_r)

    src, dst, gate, blk_e, nused = _route(logits[:, :N_EXPERTS], MOE_BM)
    gates_b = jnp.broadcast_to(gate[:, None], (gate.shape[0], LANE))
    ys = _moe_call(blk_e, nused, src, dst, gates_b,
                   w_gate[0], b_gate[0][:, None, :], w_up[0], b_up[0][:, None, :],
                   w_down[0], b_down[0][:, None, :], hf)

    outs, tok0 = [], 0
    for b, t in shapes:
        outs.append(_final_call(x2, ys, row(g_final), tok0, b * t).reshape(b, t, d))
        tok0 += b * t
    return tuple(outs)
```

```python
import functools

import numpy as np
import jax
import jax.numpy as jnp
from jax import lax
from jax.experimental import pallas as pl
from jax.experimental.pallas import tpu as pltpu

F32 = jnp.float32
BF16 = jnp.bfloat16

EPS = 1e-6
MLA_HEADS = 8
Q_LORA = 256
KV_LORA = 128
QK_NOPE = 64
ROPE_DIM = 32
QK_HEAD = QK_NOPE + ROPE_DIM
V_HEAD = 64
ROPE_THETA = 10000.0
NA_HEADS = 8
NA_HEAD_DIM = 64
NA_WIDTH = NA_HEADS * NA_HEAD_DIM
MLA_WIDTH = MLA_HEADS * V_HEAD
GRID_W = 64
NA_KH = 8
NA_KW = 16
N_EXPERTS = 32
TOP_K = 4
SWIGLU_LIMIT = 7.0
SWIGLU_ALPHA = 1.702

LANE = 128
SUB = 8
VMEM_LIMIT = 56 * 1024 * 1024

TOK_TILE = 512
MLA_TQ = 256
LOG2E = 1.4426950408889634
NA_RQ = 4
NA_WROWS = 12
MOE_BM = 512
FIN_TM = 256
ID_BITS = 18
NEG = -1e30

IN_COLS = 3 * NA_WIDTH + Q_LORA + KV_LORA + 2 * LANE


def _rms(x):
    return x * lax.rsqrt(jnp.mean(x * x, axis=-1, keepdims=True) + EPS)


def _dot(a, b):
    return jnp.dot(a, b, preferred_element_type=F32)


def _dot_nt(a, b):
    return lax.dot_general(a, b, (((1,), (1,)), ((), ())), preferred_element_type=F32)


def _cparams(sem):
    return pltpu.CompilerParams(dimension_semantics=sem, vmem_limit_bytes=VMEM_LIMIT)


def _group_tile(xa_ref, xb_ref, n0):
    return jnp.where(pl.program_id(0) < n0, xa_ref[...], xb_ref[...])


def _group_specs(tm, d, n0):
    return [pl.BlockSpec((tm, d), lambda i: (jnp.minimum(i, n0 - 1), 0)),
            pl.BlockSpec((tm, d), lambda i: (jnp.maximum(i - n0, 0), 0))]


def _proj_kernel(xa_ref, xb_ref, g_ref, win_ref, gq_ref, wqa_ref, wqb_ref, gkv_ref, wuk_ref, wuv_ref, tab_ref,
                 q_ref, k_ref, v_ref, naq_ref, nak_ref, nav_ref, *, n0):
    h = (_rms(_group_tile(xa_ref, xb_ref, n0)) * g_ref[...]).astype(BF16)
    proj = _dot(h, win_ref[...])
    o = 0
    naq_ref[...] = (proj[:, o:o + NA_WIDTH] * (NA_HEAD_DIM ** -0.5 * LOG2E)).astype(BF16)
    o += NA_WIDTH
    nak_ref[...] = proj[:, o:o + NA_WIDTH].astype(BF16)
    o += NA_WIDTH
    nav_ref[...] = proj[:, o:o + NA_WIDTH].astype(BF16)
    o += NA_WIDTH
    cq = (_rms(proj[:, o:o + Q_LORA]) * gq_ref[...]).astype(BF16)
    o += Q_LORA
    ckv = (_rms(proj[:, o:o + KV_LORA]) * gkv_ref[...]).astype(BF16)
    o += KV_LORA
    kpe_a = proj[:, o:o + LANE]
    kpe_b = proj[:, o + LANE:o + 2 * LANE]

    tab = tab_ref[...]
    caq, sbq = tab[:, 0:LANE], tab[:, LANE:2 * LANE]
    cak, sbk = tab[:, 2 * LANE:3 * LANE], tab[:, 3 * LANE:4 * LANE]

    qa = _dot(cq, wqa_ref[...])
    qb = _dot(cq, wqb_ref[...])
    kn = _dot(ckv, wuk_ref[...])
    kpe = kpe_a * cak + kpe_b * sbk
    for hd in range(MLA_HEADS):
        sl = slice(hd * LANE, (hd + 1) * LANE)
        q_ref[:, sl] = (qa[:, sl] * caq + qb[:, sl] * sbq).astype(BF16)
        k_ref[:, sl] = (kn[:, sl] + kpe).astype(BF16)
    v_ref[...] = _dot(ckv, wuv_ref[...]).astype(BF16)


def _proj_call(xa, xb, g_attn, win_r, g_q, wqa, wqb, g_kv, wuk, wuv, tab, pos_blocks):
    d = xa.shape[1]
    n = xa.shape[0] + xb.shape[0]
    tm = TOK_TILE
    (n0, pb0), (_, pb1) = pos_blocks

    def tab_map(i):
        return (jnp.where(i < n0, i % pb0, (i - n0) % pb1), 0)

    row = lambda i: (i, 0)
    fix = lambda i: (0, 0)
    out_w = (MLA_HEADS * LANE, MLA_HEADS * LANE, MLA_WIDTH, NA_WIDTH, NA_WIDTH, NA_WIDTH)
    return pl.pallas_call(
        functools.partial(_proj_kernel, n0=n0),
        grid=(n // tm,),
        in_specs=_group_specs(tm, d, n0) + [
            pl.BlockSpec((1, d), fix),
            pl.BlockSpec(win_r.shape, fix),
            pl.BlockSpec((1, Q_LORA), fix),
            pl.BlockSpec(wqa.shape, fix),
            pl.BlockSpec(wqb.shape, fix),
            pl.BlockSpec((1, KV_LORA), fix),
            pl.BlockSpec(wuk.shape, fix),
            pl.BlockSpec(wuv.shape, fix),
            pl.BlockSpec((tm, 4 * LANE), tab_map),
        ],
        out_specs=[pl.BlockSpec((tm, w), row) for w in out_w],
        out_shape=[jax.ShapeDtypeStruct((n, w), BF16) for w in out_w],
        compiler_params=_cparams(("parallel",)),
        name="proj",
    )(xa, xb, g_attn, win_r, g_q, wqa, wqb, g_kv, wuk, wuv, tab)


def _softmax_pv(s, v_ones):
    p = jnp.exp2(s - jnp.max(s, axis=-1, keepdims=True))
    acc = _dot(p.astype(BF16), v_ones)
    return acc[:, :LANE] / acc[:, LANE:]


def _mla_kernel(q_ref, k_ref, v_ref, prev_ref, o_ref):
    del prev_ref
    tq, t = q_ref.shape[1], k_ref.shape[1]
    low = lax.broadcasted_iota(jnp.int32, (tq, LANE), 1) < V_HEAD
    ones = jnp.ones((t, LANE), BF16)
    for pp in range(MLA_HEADS // 2):
        v_ones = jnp.concatenate([v_ref[0, :, pp * LANE:(pp + 1) * LANE], ones], axis=1)
        res = []
        for h in (2 * pp, 2 * pp + 1):
            s = _dot_nt(q_ref[0, :, h * LANE:(h + 1) * LANE], k_ref[0, :, h * LANE:(h + 1) * LANE])
            res.append(_softmax_pv(s, v_ones))
        o_ref[0, :, pp * LANE:(pp + 1) * LANE] = jnp.where(low, res[0], res[1]).astype(o_ref.dtype)


def _mla_call(q, k, v, nb, t, boff, prev):
    tq = min(MLA_TQ, t)
    return pl.pallas_call(
        _mla_kernel,
        grid=(nb, t // tq),
        in_specs=[
            pl.BlockSpec((1, tq, MLA_HEADS * LANE), lambda b, i: (b + boff, i, 0)),
            pl.BlockSpec((1, t, MLA_HEADS * LANE), lambda b, i: (b + boff, 0, 0)),
            pl.BlockSpec((1, t, MLA_WIDTH), lambda b, i: (b + boff, 0, 0)),
            pl.BlockSpec(memory_space=pl.ANY),
        ],
        out_specs=pl.BlockSpec((1, tq, MLA_WIDTH), lambda b, i: (b + boff, i, 0)),
        out_shape=jax.ShapeDtypeStruct(prev.shape, prev.dtype),
        input_output_aliases={3: 0},
        compiler_params=_cparams(("parallel", "arbitrary")),
        name="mla",
    )(q, k, v, prev)


def _na_base(i, rows):
    return jnp.clip(i * NA_RQ - NA_KH // 2, 0, rows - NA_WROWS)


def _na_kernel(q_ref, k_ref, v_ref, b_ref, prev_ref, o_ref):
    del prev_ref
    nq = q_ref.shape[1]
    rows = k_ref.shape[1] // GRID_W
    nk = NA_WROWS * GRID_W
    off = pl.multiple_of(_na_base(pl.program_id(1), rows) * GRID_W, GRID_W)
    lane = lax.broadcasted_iota(jnp.int32, (nq, LANE), 1)
    low = lane < NA_HEAD_DIM
    ones = jnp.ones((nk, LANE), BF16)
    for p in range(NA_HEADS // 2):
        sl = slice(p * LANE, (p + 1) * LANE)
        q2 = q_ref[0, :, sl]
        kw = k_ref[0, pl.ds(off, nk), sl]
        v_ones = jnp.concatenate([v_ref[0, pl.ds(off, nk), sl], ones], axis=1)
        outs = []
        for hh in range(2):
            qm = jnp.where(low, q2, 0) if hh == 0 else jnp.where(low, 0, q2)
            s = _dot_nt(qm, kw) + b_ref[0, 2 * p + hh]
            outs.append(_softmax_pv(s, v_ones))
        o_ref[0, :, sl] = jnp.where(low, outs[0], outs[1]).astype(o_ref.dtype)


def _na_call(q, k, v, bias, nb, t, boff, prev):
    rows = t // GRID_W
    nq = NA_RQ * GRID_W
    nk = NA_WROWS * GRID_W

    def bias_map(b, i):
        return ((i * NA_RQ - _na_base(i, rows)) // (NA_KH // 2), 0, 0, 0)

    return pl.pallas_call(
        _na_kernel,
        grid=(nb, rows // NA_RQ),
        in_specs=[
            pl.BlockSpec((1, nq, NA_WIDTH), lambda b, i: (b + boff, i, 0)),
            pl.BlockSpec((1, t, NA_WIDTH), lambda b, i: (b + boff, 0, 0)),
            pl.BlockSpec((1, t, NA_WIDTH), lambda b, i: (b + boff, 0, 0)),
            pl.BlockSpec((1, NA_HEADS, nq, nk), bias_map),
            pl.BlockSpec(memory_space=pl.ANY),
        ],
        out_specs=pl.BlockSpec((1, nq, NA_WIDTH), lambda b, i: (b + boff, i, 0)),
        out_shape=jax.ShapeDtypeStruct(prev.shape, prev.dtype),
        input_output_aliases={4: 0},
        compiler_params=_cparams(("parallel", "arbitrary")),
        name="natten",
    )(q, k, v, bias, prev)


def _na_bias_table(rpb):
    rows = 32
    c = np.arange(GRID_W)
    cstart = np.clip(c - NA_KW // 2, 0, GRID_W - NA_KW)
    col_ok = (c[None, :] >= cstart[:, None]) & (c[None, :] < cstart[:, None] + NA_KW)
    cidx = np.clip(c[None, :] - c[:, None] + NA_KW - 1, 0, 2 * NA_KW - 2)
    onehot = (cidx.reshape(-1)[None, :] == np.arange(2 * NA_KW - 1)[:, None]).astype(np.float32)
    tabs = []
    for i0 in (0, NA_RQ, rows - NA_RQ):
        base = int(np.clip(i0 - NA_KH // 2, 0, rows - NA_WROWS))
        qi = i0 + np.arange(NA_RQ)
        start = np.clip(qi - NA_KH // 2, 0, rows - NA_KH)
        kr = base + np.arange(NA_WROWS)
        row_ok = (kr[None, :] >= start[:, None]) & (kr[None, :] < start[:, None] + NA_KH)
        ridx = np.clip(kr[None, :] - qi[:, None] + NA_KH - 1, 0, 2 * NA_KH - 2)
        by_row = rpb[:, ridx, :].astype(F32)
        vals = jnp.einsum("hjrx,xq->hjrq", by_row, onehot, precision=lax.Precision.HIGHEST)
        vals = vals.reshape(NA_HEADS, NA_RQ, NA_WROWS, GRID_W, GRID_W).transpose(0, 1, 3, 2, 4)
        ok = row_ok[:, None, :, None] & col_ok[None, :, None, :]
        tab = jnp.where(ok[None], vals * LOG2E, NEG)
        tabs.append(tab.reshape(NA_HEADS, NA_RQ * GRID_W, NA_WROWS * GRID_W))
    return jnp.stack(tabs)


def _mix_kernel(a_ref, n_ref, xa_ref, xb_ref, gm_ref, gn_ref, wo_ref, gf_ref, wrh_ref, wrl_ref, br_ref,
                x2_ref, hf_ref, lg_ref, *, n0):
    an = _rms(a_ref[...].astype(F32)) * gm_ref[...]
    nn = _rms(n_ref[...].astype(F32)) * gn_ref[...]
    cat = jnp.concatenate([an, nn], axis=-1).astype(BF16)
    x2 = _group_tile(xa_ref, xb_ref, n0) + _dot(cat, wo_ref[...])
    x2_ref[...] = x2
    hf = _rms(x2) * gf_ref[...]
    tm = hf.shape[0]
    for j in range(hf.shape[1] // LANE):
        hf_ref[pl.ds(j, tm, stride=SUB), :] = hf[:, j * LANE:(j + 1) * LANE]
    hi = hf.astype(BF16)
    lo = (hf - hi.astype(F32)).astype(BF16)
    lg_ref[...] = (_dot(hi, wrh_ref[...]) + (_dot(lo, wrh_ref[...]) + _dot(hi, wrl_ref[...]))) + br_ref[...]


def _mix_call(a, nat, xa, xb, g_mla, g_na, w_out, g_ffn, wr_hi, wr_lo, b_r):
    d = xa.shape[1]
    n = xa.shape[0] + xb.shape[0]
    tm = TOK_TILE
    n0 = xa.shape[0] // tm
    row = lambda i: (i, 0)
    fix = lambda i: (0, 0)
    return pl.pallas_call(
        functools.partial(_mix_kernel, n0=n0),
        grid=(n // tm,),
        in_specs=[pl.BlockSpec((tm, MLA_WIDTH), row), pl.BlockSpec((tm, NA_WIDTH), row)] + _group_specs(tm, d, n0) + [
            pl.BlockSpec((1, MLA_WIDTH), fix),
            pl.BlockSpec((1, NA_WIDTH), fix),
            pl.BlockSpec(w_out.shape, fix),
            pl.BlockSpec((1, d), fix),
            pl.BlockSpec(wr_hi.shape, fix),
            pl.BlockSpec(wr_lo.shape, fix),
            pl.BlockSpec((1, LANE), fix),
        ],
        out_specs=[pl.BlockSpec((tm, d), row), pl.BlockSpec((tm * SUB, LANE), row), pl.BlockSpec((tm, LANE), row)],
        out_shape=[jax.ShapeDtypeStruct((n, d), F32), jax.ShapeDtypeStruct((n * SUB, LANE), F32),
                   jax.ShapeDtypeStruct((n, LANE), F32)],
        compiler_params=_cparams(("parallel",)),
        name="mix",
    )(a, nat, xa, xb, g_mla, g_na, w_out, g_ffn, wr_hi, wr_lo, b_r)


def _gather_tiles(idx_ref, src_hbm, buf, sem, count):
    for r in range(count):
        src = src_hbm.at[pl.ds(pl.multiple_of(idx_ref[0, 0, r], SUB), SUB)]
        pltpu.make_async_copy(src, buf.at[pl.ds(r * SUB, SUB)], sem).start(priority=r % 2)


def _moe_kernel(be_ref, nu_ref, src_ref, srcn_ref, wgf_ref, bg_ref, wuf_ref, bu_ref, wdf_ref, bd_ref, hf_hbm,
                y_ref, xbuf, wg_ref, wu_ref, wd_ref, gsem):
    i = pl.program_id(0)
    nu = nu_ref[0]
    bm = xbuf.shape[1] // SUB
    nchunk = wd_ref.shape[1] // LANE
    blk = pl.ds(0, bm * SUB)

    def step(s):
        @pl.when(i + 1 < nu)
        def _():
            _gather_tiles(srcn_ref, hf_hbm, xbuf.at[1 - s], gsem.at[1 - s], bm)

        @pl.when(i >= nu)
        def _():
            y_ref[...] = jnp.zeros(y_ref.shape, F32)

        @pl.when(i < nu)
        def _():
            pltpu.make_async_copy(hf_hbm.at[blk], xbuf.at[s], gsem.at[s]).wait()
            x = jnp.concatenate([xbuf[s, pl.ds(j, bm, stride=SUB), :] for j in range(nchunk)], axis=1).astype(BF16)
            g = _dot(x, wg_ref[...]) + bg_ref[0]
            u = _dot(x, wu_ref[...]) + bu_ref[0]
            g = jnp.minimum(g, SWIGLU_LIMIT)
            u = jnp.clip(u, -SWIGLU_LIMIT, SWIGLU_LIMIT)
            h = (u + 1.0) * (g * jax.nn.sigmoid(SWIGLU_ALPHA * g))
            y = _dot(h.astype(BF16), wd_ref[...]) + bd_ref[0]
            for j in range(nchunk):
                y_ref[pl.ds(j, bm, stride=SUB), :] = y[:, j * LANE:(j + 1) * LANE]

    @pl.when(i == 0)
    def _():
        _gather_tiles(src_ref, hf_hbm, xbuf.at[0], gsem.at[0], bm)

    blk_i = jnp.minimum(i, nu - 1)

    @pl.when((i == 0) | ((i < nu) & (be_ref[blk_i] != be_ref[jnp.maximum(blk_i - 1, 0)])))
    def _():
        wg_ref[...] = wgf_ref[0].astype(BF16)
        wu_ref[...] = wuf_ref[0].astype(BF16)
        wd_ref[...] = wdf_ref[0].astype(BF16)

    for s in range(2):
        pl.when(lax.rem(i, 2) == s)(functools.partial(step, s))


def _moe_call(blk_e, nused, src, wg, bg, wu, bu, wd, bd, hf):
    nblk, _, bm = src.shape
    d = wg.shape[1]
    dff = wg.shape[2]
    assert d == SUB * LANE and hf.shape[1] == LANE
    last = lambda i, nu: jnp.minimum(i, nu[0] - 1)
    cur3 = lambda i, be, nu: (last(i, nu), 0, 0)
    nxt3 = lambda i, be, nu: (last(i + 1, nu), 0, 0)
    wmap = lambda i, be, nu: (be[last(i, nu)], 0, 0)
    smem = functools.partial(pl.BlockSpec, (1, 1, bm), memory_space=pltpu.SMEM)
    grid_spec = pltpu.PrefetchScalarGridSpec(
        num_scalar_prefetch=2,
        grid=(nblk,),
        in_specs=[
            smem(cur3), smem(nxt3),
            pl.BlockSpec((1, d, dff), wmap),
            pl.BlockSpec((1, 1, dff), wmap),
            pl.BlockSpec((1, d, dff), wmap),
            pl.BlockSpec((1, 1, dff), wmap),
            pl.BlockSpec((1, dff, d), wmap),
            pl.BlockSpec((1, 1, d), wmap),
            pl.BlockSpec(memory_space=pl.ANY),
        ],
        out_specs=pl.BlockSpec((bm * SUB, LANE), lambda i, be, nu: (i, 0)),
        scratch_shapes=[
            pltpu.VMEM((2, bm * SUB, LANE), F32),
            pltpu.VMEM((d, dff), BF16),
            pltpu.VMEM((d, dff), BF16),
            pltpu.VMEM((dff, d), BF16),
            pltpu.SemaphoreType.DMA((2,)),
        ],
    )
    return pl.pallas_call(
        _moe_kernel,
        grid_spec=grid_spec,
        out_shape=jax.ShapeDtypeStruct((nblk * bm * SUB, LANE), F32),
        compiler_params=_cparams(("arbitrary",)),
        name="moe",
    )(blk_e, nused, src, src, wg, bg, wu, bu, wd, bd, hf)


def _final_kernel(pos_ref, posn_ref, x_ref, gt_ref, g_ref, ys_hbm, o_ref, ybuf, gsem, *, nt):
    i = pl.program_id(0)
    tm = x_ref.shape[0]
    cnt = TOP_K * tm

    def step(s):
        @pl.when(i + 1 < nt)
        def _():
            _gather_tiles(posn_ref, ys_hbm, ybuf.at[1 - s], gsem.at[1 - s], cnt)

        pltpu.make_async_copy(ys_hbm.at[pl.ds(0, cnt * SUB)], ybuf.at[s], gsem.at[s]).wait()
        gates = gt_ref[...]
        gk = [jnp.broadcast_to(gates[:, k:k + 1], (tm, LANE)) for k in range(TOP_K)]
        chunks = []
        for j in range(x_ref.shape[1] // LANE):
            parts = [gk[k] * ybuf[s, pl.ds(k * tm * SUB + j, tm, stride=SUB), :] for k in range(TOP_K)]
            chunks.append((parts[0] + parts[1]) + (parts[2] + parts[3]))
        o_ref[...] = _rms(x_ref[...] + jnp.concatenate(chunks, axis=1)) * g_ref[...]

    @pl.when(i == 0)
    def _():
        _gather_tiles(pos_ref, ys_hbm, ybuf.at[0], gsem.at[0], cnt)

    for s in range(2):
        pl.when(lax.rem(i, 2) == s)(functools.partial(step, s))


def _final_call(x2, ys, pos_tiles, gates, g_final, tok0, ntok):
    d = x2.shape[1]
    assert TOP_K == 4 and d == SUB * LANE
    tm = FIN_TM
    assert tok0 % tm == 0 and ntok % tm == 0
    off, nt = tok0 // tm, ntok // tm
    smem = functools.partial(pl.BlockSpec, (1, 1, TOP_K * tm), memory_space=pltpu.SMEM)
    return pl.pallas_call(
        functools.partial(_final_kernel, nt=nt),
        grid=(nt,),
        in_specs=[
            smem(lambda i: (off + i, 0, 0)),
            smem(lambda i: (off + jnp.minimum(i + 1, nt - 1), 0, 0)),
            pl.BlockSpec((tm, d), lambda i: (off + i, 0)),
            pl.BlockSpec((tm, TOP_K), lambda i: (off + i, 0)),
            pl.BlockSpec((1, d), lambda i: (0, 0)),
            pl.BlockSpec(memory_space=pl.ANY),
        ],
        out_specs=pl.BlockSpec((tm, d), lambda i: (i, 0)),
        out_shape=jax.ShapeDtypeStruct((ntok, d), F32),
        scratch_shapes=[pltpu.VMEM((2, TOP_K * tm * SUB, LANE), F32), pltpu.SemaphoreType.DMA((2,))],
        compiler_params=_cparams(("arbitrary",)),
        name="final",
    )(pos_tiles, pos_tiles, x2, gates, g_final, ys)


def _prep_weights(w_in, w_uq, w_ukv, t_max):
    d = w_in.shape[0]
    o = np.cumsum((0, Q_LORA, KV_LORA, ROPE_DIM, NA_WIDTH, NA_WIDTH, NA_WIDTH))
    w_cq, w_ckv, w_kpe = w_in[:, o[0]:o[1]], w_in[:, o[1]:o[2]], w_in[:, o[2]:o[3]]
    w_naq, w_nak, w_nav = w_in[:, o[3]:o[4]], w_in[:, o[4]:o[5]], w_in[:, o[5]:o[6]]
    half = ROPE_DIM // 2
    swap = np.concatenate([np.arange(half, ROPE_DIM), np.arange(half)])
    zpad = lambda w, lo: jnp.pad(w, ((0, 0), (lo, LANE - lo - w.shape[1])))
    win_r = jnp.concatenate([w_naq, w_nak, w_nav, w_cq, w_ckv,
                             zpad(w_kpe, QK_NOPE), zpad(w_kpe[:, swap], QK_NOPE)], axis=1).astype(BF16)

    uq = w_uq.reshape(Q_LORA, MLA_HEADS, QK_HEAD)
    wqa = jnp.pad(uq, ((0, 0), (0, 0), (0, LANE - QK_HEAD))).reshape(Q_LORA, MLA_HEADS * LANE).astype(BF16)
    wqb = jnp.pad(uq[:, :, QK_NOPE:][:, :, swap], ((0, 0), (0, 0), (QK_NOPE, LANE - QK_HEAD)))
    wqb = wqb.reshape(Q_LORA, MLA_HEADS * LANE).astype(BF16)
    ukv = w_ukv.reshape(KV_LORA, MLA_HEADS, QK_NOPE + V_HEAD)
    wuk = jnp.pad(ukv[:, :, :QK_NOPE], ((0, 0), (0, 0), (0, LANE - QK_NOPE)))
    wuk = wuk.reshape(KV_LORA, MLA_HEADS * LANE).astype(BF16)
    wuv = ukv[:, :, QK_NOPE:].reshape(KV_LORA, MLA_WIDTH).astype(BF16)

    pos = jnp.arange(t_max, dtype=F32)
    inv = 1.0 / (ROPE_THETA ** (jnp.arange(0, ROPE_DIM, 2, dtype=F32) / ROPE_DIM))
    ang = pos[:, None] * inv[None, :]
    cos, sin = jnp.cos(ang), jnp.sin(ang)
    c32 = jnp.concatenate([cos, cos], axis=1)
    s32 = jnp.concatenate([-sin, sin], axis=1)
    z = lambda w: jnp.zeros((t_max, w), F32)
    cak = jnp.concatenate([z(QK_NOPE), c32, z(LANE - QK_HEAD)], axis=1)
    sbk = jnp.concatenate([z(QK_NOPE), s32, z(LANE - QK_HEAD)], axis=1)
    scale = QK_HEAD ** -0.5 * LOG2E
    caq = scale * jnp.concatenate([jnp.ones((t_max, QK_NOPE), F32), c32, z(LANE - QK_HEAD)], axis=1)
    tab = jnp.concatenate([caq, scale * sbk, cak, sbk], axis=1)
    del d
    return win_r, wqa, wqb, wuk, wuv, tab


def _route(logits, bm, tm):
    n = logits.shape[0]
    a = n * TOP_K
    assert a <= (1 << ID_BITS) and (N_EXPERTS << ID_BITS) < 2 ** 31
    nblk = a // bm + N_EXPERTS
    top_v, top_i = lax.top_k(logits, TOP_K)
    gates = jax.nn.softmax(top_v, axis=-1)
    flat_e = top_i.reshape(-1).astype(jnp.int32)
    ids = jnp.arange(a, dtype=jnp.int32)
    key = jnp.sort(flat_e * (1 << ID_BITS) + ids)
    order, se = key & ((1 << ID_BITS) - 1), key >> ID_BITS
    experts = jnp.arange(N_EXPERTS, dtype=jnp.int32)
    counts = jnp.sum((flat_e[None, :] == experts[:, None]).astype(jnp.int32), axis=1)
    start = jnp.cumsum(counts) - counts
    padded = ((counts + bm - 1) // bm) * bm
    pad_end = jnp.cumsum(padded)
    pad_start = pad_end - padded
    blk0 = jnp.arange(nblk, dtype=jnp.int32) * bm
    blk_e = jnp.minimum(jnp.sum((pad_end[None, :] <= blk0[:, None]).astype(jnp.int32), axis=1), N_EXPERTS - 1)
    nused = (pad_end[-1:] // bm).astype(jnp.int32)
    rank = (blk0 - pad_start[blk_e])[:, None] + jnp.arange(bm, dtype=jnp.int32)[None, :]
    valid = rank < counts[blk_e][:, None]
    aid = order[jnp.clip(start[blk_e][:, None] + rank, 0, a - 1)]
    src = jnp.where(valid, aid // TOP_K, 0)
    slot_sorted = (pad_start - start)[se] + ids
    _, pos = lax.sort((order, slot_sorted), num_keys=1)
    pos_tiles = (pos * SUB).reshape(n // tm, tm, TOP_K).transpose(0, 2, 1).reshape(n // tm, 1, TOP_K * tm)
    return (src * SUB)[:, None, :], blk_e.astype(jnp.int32), nused, pos_tiles, gates


def kernel(x_prompt, x_sample, g_attn_norm, w_in, g_q_lora, w_uq, g_kv_lora, w_ukv, na_rpb, g_out_mla,
           g_out_na, w_out, g_ffn_norm, w_router, b_router, w_gate, b_gate, w_up, b_up, w_down, b_down,
           g_final):
    assert g_attn_norm.shape[0] == 1, "single-layer trunk"
    d = x_prompt.shape[-1]
    groups = [x_prompt, x_sample]
    shapes = [(g.shape[0], g.shape[1]) for g in groups]
    for _, t in shapes:
        assert t % TOK_TILE == 0 and t % GRID_W == 0 and (t // GRID_W) % NA_RQ == 0
        assert t // GRID_W >= NA_WROWS + NA_RQ
    xa, xb = (g.reshape(-1, d) for g in groups)
    n = xa.shape[0] + xb.shape[0]
    t_max = max(t for _, t in shapes)

    win_r, wqa, wqb, wuk, wuv, tab = _prep_weights(w_in[0], w_uq[0], w_ukv[0], t_max)
    pos_blocks = [(b * t // TOK_TILE, t // TOK_TILE) for b, t in shapes]
    row = lambda v: v.reshape(1, -1)
    q, k, v, naq, nak, nav = _proj_call(xa, xb, row(g_attn_norm[0]), win_r, row(g_q_lora[0]), wqa, wqb,
                                        row(g_kv_lora[0]), wuk, wuv, tab, pos_blocks)

    bias = _na_bias_table(na_rpb[0])
    a_all = jnp.zeros((n, MLA_WIDTH), BF16)
    n_all = jnp.zeros((n, NA_WIDTH), BF16)
    tok0 = 0
    for b, t in shapes:
        assert tok0 % t == 0 and n % t == 0
        boff = tok0 // t
        view = lambda arr: arr.reshape(n // t, t, arr.shape[-1])
        a_all = _mla_call(view(q), view(k), view(v), b, t, boff, view(a_all)).reshape(n, MLA_WIDTH)
        n_all = _na_call(view(naq), view(nak), view(nav), bias, b, t, boff, view(n_all)).reshape(n, NA_WIDTH)
        tok0 += b * t

    wr = jnp.pad(w_router[0], ((0, 0), (0, LANE - N_EXPERTS)))
    wr_hi = wr.astype(BF16)
    wr_lo = (wr - wr_hi.astype(F32)).astype(BF16)
    b_r = jnp.pad(b_router[0], (0, LANE - N_EXPERTS)).reshape(1, LANE)
    x2, hf, logits = _mix_call(a_all, n_all, xa, xb, row(g_out_mla[0]), row(g_out_na[0]), w_out[0].astype(BF16),
                               row(g_ffn_norm[0]), wr_hi, wr_lo, b_r)

    src, blk_e, nused, pos_tiles, gates = _route(logits[:, :N_EXPERTS], MOE_BM, FIN_TM)
    ys = _moe_call(blk_e, nused, src, w_gate[0], b_gate[0][:, None, :], w_up[0], b_up[0][:, None, :],
                   w_down[0], b_down[0][:, None, :], hf)

    outs, tok0 = [], 0
    for b, t in shapes:
        outs.append(_final_call(x2, ys, pos_tiles, gates, row(g_final), tok0, b * t).reshape(b, t, d))
        tok0 += b * t
    return tuple(outs)
```

```python
import functools

import numpy as np
import jax
import jax.numpy as jnp
from jax import lax
from jax.experimental import pallas as pl
from jax.experimental.pallas import tpu as pltpu

F32 = jnp.float32
BF16 = jnp.bfloat16

EPS = 1e-6
MLA_HEADS = 8
Q_LORA = 256
KV_LORA = 128
QK_NOPE = 64
ROPE_DIM = 32
QK_HEAD = QK_NOPE + ROPE_DIM
V_HEAD = 64
ROPE_THETA = 10000.0
NA_HEADS = 8
NA_HEAD_DIM = 64
NA_WIDTH = NA_HEADS * NA_HEAD_DIM
MLA_WIDTH = MLA_HEADS * V_HEAD
GRID_W = 64
NA_KH = 8
NA_KW = 16
N_EXPERTS = 32
TOP_K = 4
SWIGLU_LIMIT = 7.0
SWIGLU_ALPHA = 1.702

LANE = 128
SUB = 8
VMEM_LIMIT = 56 * 1024 * 1024

TOK_TILE = 512
MLA_TQ = 256
LOG2E = 1.4426950408889634
NA_RQ = 4
NA_WROWS = 12
MOE_BM = 512
FIN_TM = 256
NEG = -1e30

IN_COLS = 3 * NA_WIDTH + Q_LORA + KV_LORA + 2 * LANE


def _rms(x):
    return x * lax.rsqrt(jnp.mean(x * x, axis=-1, keepdims=True) + EPS)


def _dot(a, b):
    return jnp.dot(a, b, preferred_element_type=F32)


def _dot_nt(a, b):
    return lax.dot_general(a, b, (((1,), (1,)), ((), ())), preferred_element_type=F32)


def _cparams(sem):
    return pltpu.CompilerParams(dimension_semantics=sem, vmem_limit_bytes=VMEM_LIMIT)


def _group_tile(xa_ref, xb_ref, n0):
    return jnp.where(pl.program_id(0) < n0, xa_ref[...], xb_ref[...])


def _group_specs(tm, d, n0):
    return [pl.BlockSpec((tm, d), lambda i: (jnp.minimum(i, n0 - 1), 0)),
            pl.BlockSpec((tm, d), lambda i: (jnp.maximum(i - n0, 0), 0))]


def _proj_kernel(xa_ref, xb_ref, g_ref, win_ref, gq_ref, wqa_ref, wqb_ref, gkv_ref, wuk_ref, wuv_ref, tab_ref,
                 q_ref, k_ref, v_ref, naq_ref, nak_ref, nav_ref, *, n0):
    h = (_rms(_group_tile(xa_ref, xb_ref, n0)) * g_ref[...]).astype(BF16)
    proj = _dot(h, win_ref[...])
    o = 0
    naq_ref[...] = (proj[:, o:o + NA_WIDTH] * (NA_HEAD_DIM ** -0.5 * LOG2E)).astype(BF16)
    o += NA_WIDTH
    nak_ref[...] = proj[:, o:o + NA_WIDTH].astype(BF16)
    o += NA_WIDTH
    nav_ref[...] = proj[:, o:o + NA_WIDTH].astype(BF16)
    o += NA_WIDTH
    cq = (_rms(proj[:, o:o + Q_LORA]) * gq_ref[...]).astype(BF16)
    o += Q_LORA
    ckv = (_rms(proj[:, o:o + KV_LORA]) * gkv_ref[...]).astype(BF16)
    o += KV_LORA
    kpe_a = proj[:, o:o + LANE]
    kpe_b = proj[:, o + LANE:o + 2 * LANE]

    tab = tab_ref[...]
    caq, sbq = tab[:, 0:LANE], tab[:, LANE:2 * LANE]
    cak, sbk = tab[:, 2 * LANE:3 * LANE], tab[:, 3 * LANE:4 * LANE]

    qa = _dot(cq, wqa_ref[...])
    qb = _dot(cq, wqb_ref[...])
    kn = _dot(ckv, wuk_ref[...])
    kpe = kpe_a * cak + kpe_b * sbk
    for hd in range(MLA_HEADS):
        sl = slice(hd * LANE, (hd + 1) * LANE)
        q_ref[:, sl] = (qa[:, sl] * caq + qb[:, sl] * sbq).astype(BF16)
        k_ref[:, sl] = (kn[:, sl] + kpe).astype(BF16)
    v_ref[...] = _dot(ckv, wuv_ref[...]).astype(BF16)


def _proj_call(xa, xb, g_attn, win_r, g_q, wqa, wqb, g_kv, wuk, wuv, tab, pos_blocks):
    d = xa.shape[1]
    n = xa.shape[0] + xb.shape[0]
    tm = TOK_TILE
    (n0, pb0), (_, pb1) = pos_blocks

    def tab_map(i):
        return (jnp.where(i < n0, i % pb0, (i - n0) % pb1), 0)

    row = lambda i: (i, 0)
    fix = lambda i: (0, 0)
    out_w = (MLA_HEADS * LANE, MLA_HEADS * LANE, MLA_WIDTH, NA_WIDTH, NA_WIDTH, NA_WIDTH)
    return pl.pallas_call(
        functools.partial(_proj_kernel, n0=n0),
        grid=(n // tm,),
        in_specs=_group_specs(tm, d, n0) + [
            pl.BlockSpec((1, d), fix),
            pl.BlockSpec(win_r.shape, fix),
            pl.BlockSpec((1, Q_LORA), fix),
            pl.BlockSpec(wqa.shape, fix),
            pl.BlockSpec(wqb.shape, fix),
            pl.BlockSpec((1, KV_LORA), fix),
            pl.BlockSpec(wuk.shape, fix),
            pl.BlockSpec(wuv.shape, fix),
            pl.BlockSpec((tm, 4 * LANE), tab_map),
        ],
        out_specs=[pl.BlockSpec((tm, w), row) for w in out_w],
        out_shape=[jax.ShapeDtypeStruct((n, w), BF16) for w in out_w],
        compiler_params=_cparams(("parallel",)),
        name="proj",
    )(xa, xb, g_attn, win_r, g_q, wqa, wqb, g_kv, wuk, wuv, tab)


def _softmax_pv(s, v_ones):
    p = jnp.exp2(s - jnp.max(s, axis=-1, keepdims=True))
    acc = _dot(p.astype(BF16), v_ones)
    return acc[:, :LANE] / acc[:, LANE:]


def _mla_kernel(q_ref, k_ref, v_ref, prev_ref, o_ref):
    del prev_ref
    tq, t = q_ref.shape[1], k_ref.shape[1]
    low = lax.broadcasted_iota(jnp.int32, (tq, LANE), 1) < V_HEAD
    ones = jnp.ones((t, LANE), BF16)
    for pp in range(MLA_HEADS // 2):
        v_ones = jnp.concatenate([v_ref[0, :, pp * LANE:(pp + 1) * LANE], ones], axis=1)
        res = []
        for h in (2 * pp, 2 * pp + 1):
            s = _dot_nt(q_ref[0, :, h * LANE:(h + 1) * LANE], k_ref[0, :, h * LANE:(h + 1) * LANE])
            res.append(_softmax_pv(s, v_ones))
        o_ref[0, :, pp * LANE:(pp + 1) * LANE] = jnp.where(low, res[0], res[1]).astype(o_ref.dtype)


def _mla_call(q, k, v, nb, t, boff, prev):
    tq = min(MLA_TQ, t)
    return pl.pallas_call(
        _mla_kernel,
        grid=(nb, t // tq),
        in_specs=[
            pl.BlockSpec((1, tq, MLA_HEADS * LANE), lambda b, i: (b + boff, i, 0)),
            pl.BlockSpec((1, t, MLA_HEADS * LANE), lambda b, i: (b + boff, 0, 0)),
            pl.BlockSpec((1, t, MLA_WIDTH), lambda b, i: (b + boff, 0, 0)),
            pl.BlockSpec(memory_space=pl.ANY),
        ],
        out_specs=pl.BlockSpec((1, tq, MLA_WIDTH), lambda b, i: (b + boff, i, 0)),
        out_shape=jax.ShapeDtypeStruct(prev.shape, prev.dtype),
        input_output_aliases={3: 0},
        compiler_params=_cparams(("parallel", "arbitrary")),
        name="mla",
    )(q, k, v, prev)


def _na_base(i, rows):
    return jnp.clip(i * NA_RQ - NA_KH // 2, 0, rows - NA_WROWS)


def _na_kernel(q_ref, k_ref, v_ref, b_ref, prev_ref, o_ref):
    del prev_ref
    nq = q_ref.shape[1]
    rows = k_ref.shape[1] // GRID_W
    nk = NA_WROWS * GRID_W
    off = pl.multiple_of(_na_base(pl.program_id(1), rows) * GRID_W, GRID_W)
    lane = lax.broadcasted_iota(jnp.int32, (nq, LANE), 1)
    low = lane < NA_HEAD_DIM
    ones = jnp.ones((nk, LANE), BF16)
    for p in range(NA_HEADS // 2):
        sl = slice(p * LANE, (p + 1) * LANE)
        q2 = q_ref[0, :, sl]
        kw = k_ref[0, pl.ds(off, nk), sl]
        v_ones = jnp.concatenate([v_ref[0, pl.ds(off, nk), sl], ones], axis=1)
        outs = []
        for hh in range(2):
            qm = jnp.where(low, q2, 0) if hh == 0 else jnp.where(low, 0, q2)
            s = _dot_nt(qm, kw) + b_ref[0, 2 * p + hh]
            outs.append(_softmax_pv(s, v_ones))
        o_ref[0, :, sl] = jnp.where(low, outs[0], outs[1]).astype(o_ref.dtype)


def _na_call(q, k, v, bias, nb, t, boff, prev):
    rows = t // GRID_W
    nq = NA_RQ * GRID_W
    nk = NA_WROWS * GRID_W

    def bias_map(b, i):
        return ((i * NA_RQ - _na_base(i, rows)) // (NA_KH // 2), 0, 0, 0)

    return pl.pallas_call(
        _na_kernel,
        grid=(nb, rows // NA_RQ),
        in_specs=[
            pl.BlockSpec((1, nq, NA_WIDTH), lambda b, i: (b + boff, i, 0)),
            pl.BlockSpec((1, t, NA_WIDTH), lambda b, i: (b + boff, 0, 0)),
            pl.BlockSpec((1, t, NA_WIDTH), lambda b, i: (b + boff, 0, 0)),
            pl.BlockSpec((1, NA_HEADS, nq, nk), bias_map),
            pl.BlockSpec(memory_space=pl.ANY),
        ],
        out_specs=pl.BlockSpec((1, nq, NA_WIDTH), lambda b, i: (b + boff, i, 0)),
        out_shape=jax.ShapeDtypeStruct(prev.shape, prev.dtype),
        input_output_aliases={4: 0},
        compiler_params=_cparams(("parallel", "arbitrary")),
        name="natten",
    )(q, k, v, bias, prev)


def _na_bias_table(rpb):
    rows = 32
    c = np.arange(GRID_W)
    cstart = np.clip(c - NA_KW // 2, 0, GRID_W - NA_KW)
    col_ok = (c[None, :] >= cstart[:, None]) & (c[None, :] < cstart[:, None] + NA_KW)
    cidx = np.clip(c[None, :] - c[:, None] + NA_KW - 1, 0, 2 * NA_KW - 2)
    onehot = (cidx.reshape(-1)[None, :] == np.arange(2 * NA_KW - 1)[:, None]).astype(np.float32)
    tabs = []
    for i0 in (0, NA_RQ, rows - NA_RQ):
        base = int(np.clip(i0 - NA_KH // 2, 0, rows - NA_WROWS))
        qi = i0 + np.arange(NA_RQ)
        start = np.clip(qi - NA_KH // 2, 0, rows - NA_KH)
        kr = base + np.arange(NA_WROWS)
        row_ok = (kr[None, :] >= start[:, None]) & (kr[None, :] < start[:, None] + NA_KH)
        ridx = np.clip(kr[None, :] - qi[:, None] + NA_KH - 1, 0, 2 * NA_KH - 2)
        by_row = rpb[:, ridx, :].astype(F32)
        vals = jnp.einsum("hjrx,xq->hjrq", by_row, onehot, precision=lax.Precision.HIGHEST)
        vals = vals.reshape(NA_HEADS, NA_RQ, NA_WROWS, GRID_W, GRID_W).transpose(0, 1, 3, 2, 4)
        ok = row_ok[:, None, :, None] & col_ok[None, :, None, :]
        tab = jnp.where(ok[None], vals * LOG2E, NEG)
        tabs.append(tab.reshape(NA_HEADS, NA_RQ * GRID_W, NA_WROWS * GRID_W))
    return jnp.stack(tabs)


REC_ROWS = 16


def _route_tile(logits, tri, run):
    ne, tm = logits.shape
    row = lax.broadcasted_iota(jnp.int32, (ne, tm), 0)
    left = logits
    onehot = jnp.zeros((ne, tm), F32)
    vals, idxs = [], []
    for _ in range(TOP_K):
        m = jnp.max(left, axis=0, keepdims=True)
        idx = jnp.min(jnp.where(left == m, row, ne), axis=0, keepdims=True)
        sel = row == idx
        onehot = onehot + sel.astype(F32)
        left = jnp.where(sel, NEG, left)
        vals.append(m)
        idxs.append(idx)
    ex = [jnp.exp(v - vals[0]) for v in vals]
    den = (ex[0] + ex[1]) + (ex[2] + ex[3])
    before = _dot(onehot.astype(BF16), tri) + run
    ranks = [jnp.sum(jnp.where(row == idx, before, 0.0), axis=0, keepdims=True) for idx in idxs]
    rec = [idx.astype(F32) for idx in idxs] + ranks + [e / den for e in ex]
    rec.append(jnp.zeros((REC_ROWS - len(rec), tm), F32))
    return jnp.concatenate(rec, axis=0), jnp.sum(onehot, axis=1, keepdims=True)


def _mix_kernel(a_ref, n_ref, xa_ref, xb_ref, gm_ref, gn_ref, wo_ref, gf_ref, wrh_ref, wrl_ref, br_ref, tri_ref,
                x2_ref, hf_ref, rt_ref, cnt_ref, *, n0):
    @pl.when(pl.program_id(0) == 0)
    def _():
        cnt_ref[...] = jnp.zeros(cnt_ref.shape, F32)

    an = _rms(a_ref[...].astype(F32)) * gm_ref[...]
    nn = _rms(n_ref[...].astype(F32)) * gn_ref[...]
    cat = jnp.concatenate([an, nn], axis=-1).astype(BF16)
    x2 = _group_tile(xa_ref, xb_ref, n0) + _dot(cat, wo_ref[...])
    x2_ref[...] = x2
    hf = _rms(x2) * gf_ref[...]
    tm = hf.shape[0]
    for j in range(hf.shape[1] // LANE):
        hf_ref[pl.ds(j, tm, stride=SUB), :] = hf[:, j * LANE:(j + 1) * LANE]
    hi = hf.astype(BF16)
    lo = (hf - hi.astype(F32)).astype(BF16)
    logits = (_dot_nt(wrh_ref[...], hi) + (_dot_nt(wrh_ref[...], lo) + _dot_nt(wrl_ref[...], hi))) + br_ref[...]
    rt_ref[...], tile_cnt = _route_tile(logits, tri_ref[...], cnt_ref[:, 0:1])
    cnt_ref[...] += tile_cnt


def _mix_call(a, nat, xa, xb, g_mla, g_na, w_out, g_ffn, wr_hi, wr_lo, b_r):
    d = xa.shape[1]
    n = xa.shape[0] + xb.shape[0]
    tm = TOK_TILE
    n0 = xa.shape[0] // tm
    row = lambda i: (i, 0)
    fix = lambda i: (0, 0)
    tri = jnp.asarray(np.triu(np.ones((tm, tm), np.float32), 1), BF16)
    return pl.pallas_call(
        functools.partial(_mix_kernel, n0=n0),
        grid=(n // tm,),
        in_specs=[pl.BlockSpec((tm, MLA_WIDTH), row), pl.BlockSpec((tm, NA_WIDTH), row)] + _group_specs(tm, d, n0) + [
            pl.BlockSpec((1, MLA_WIDTH), fix),
            pl.BlockSpec((1, NA_WIDTH), fix),
            pl.BlockSpec(w_out.shape, fix),
            pl.BlockSpec((1, d), fix),
            pl.BlockSpec(wr_hi.shape, fix),
            pl.BlockSpec(wr_lo.shape, fix),
            pl.BlockSpec((N_EXPERTS, 1), fix),
            pl.BlockSpec((tm, tm), fix),
        ],
        out_specs=[pl.BlockSpec((tm, d), row), pl.BlockSpec((tm * SUB, LANE), row),
                   pl.BlockSpec((REC_ROWS, tm), lambda i: (0, i)), pl.BlockSpec((N_EXPERTS, LANE), fix)],
        out_shape=[jax.ShapeDtypeStruct((n, d), F32), jax.ShapeDtypeStruct((n * SUB, LANE), F32),
                   jax.ShapeDtypeStruct((REC_ROWS, n), F32), jax.ShapeDtypeStruct((N_EXPERTS, LANE), F32)],
        compiler_params=_cparams(("arbitrary",)),
        name="mix",
    )(a, nat, xa, xb, g_mla, g_na, w_out, g_ffn, wr_hi, wr_lo, b_r, tri)


def _gather_tiles(idx_ref, src_hbm, buf, sem, count):
    for r in range(count):
        src = src_hbm.at[pl.ds(pl.multiple_of(idx_ref[0, 0, r], SUB), SUB)]
        pltpu.make_async_copy(src, buf.at[pl.ds(r * SUB, SUB)], sem).start(priority=r % 2)


def _moe_kernel(be_ref, nu_ref, src_ref, srcn_ref, wgf_ref, bg_ref, wuf_ref, bu_ref, wdf_ref, bd_ref, hf_hbm,
                y_ref, xbuf, wg_ref, wu_ref, wd_ref, gsem):
    i = pl.program_id(0)
    nu = nu_ref[0]
    bm = xbuf.shape[1] // SUB
    nchunk = wd_ref.shape[1] // LANE
    blk = pl.ds(0, bm * SUB)

    def step(s):
        @pl.when(i + 1 < nu)
        def _():
            _gather_tiles(srcn_ref, hf_hbm, xbuf.at[1 - s], gsem.at[1 - s], bm)

        @pl.when(i >= nu)
        def _():
            y_ref[...] = jnp.zeros(y_ref.shape, F32)

        @pl.when(i < nu)
        def _():
            pltpu.make_async_copy(hf_hbm.at[blk], xbuf.at[s], gsem.at[s]).wait()
            x = jnp.concatenate([xbuf[s, pl.ds(j, bm, stride=SUB), :] for j in range(nchunk)], axis=1).astype(BF16)
            g = _dot(x, wg_ref[...]) + bg_ref[0]
            u = _dot(x, wu_ref[...]) + bu_ref[0]
            g = jnp.minimum(g, SWIGLU_LIMIT)
            u = jnp.clip(u, -SWIGLU_LIMIT, SWIGLU_LIMIT)
            h = (u + 1.0) * (g * jax.nn.sigmoid(SWIGLU_ALPHA * g))
            y = _dot(h.astype(BF16), wd_ref[...]) + bd_ref[0]
            for j in range(nchunk):
                y_ref[pl.ds(j, bm, stride=SUB), :] = y[:, j * LANE:(j + 1) * LANE]

    @pl.when(i == 0)
    def _():
        _gather_tiles(src_ref, hf_hbm, xbuf.at[0], gsem.at[0], bm)

    blk_i = jnp.minimum(i, nu - 1)

    @pl.when((i == 0) | ((i < nu) & (be_ref[blk_i] != be_ref[jnp.maximum(blk_i - 1, 0)])))
    def _():
        wg_ref[...] = wgf_ref[0].astype(BF16)
        wu_ref[...] = wuf_ref[0].astype(BF16)
        wd_ref[...] = wdf_ref[0].astype(BF16)

    for s in range(2):
        pl.when(lax.rem(i, 2) == s)(functools.partial(step, s))


def _moe_call(blk_e, nused, src, wg, bg, wu, bu, wd, bd, hf):
    nblk, _, bm = src.shape
    d = wg.shape[1]
    dff = wg.shape[2]
    assert d == SUB * LANE and hf.shape[1] == LANE
    last = lambda i, nu: jnp.minimum(i, nu[0] - 1)
    cur3 = lambda i, be, nu: (last(i, nu), 0, 0)
    nxt3 = lambda i, be, nu: (last(i + 1, nu), 0, 0)
    wmap = lambda i, be, nu: (be[last(i, nu)], 0, 0)
    smem = functools.partial(pl.BlockSpec, (1, 1, bm), memory_space=pltpu.SMEM)
    grid_spec = pltpu.PrefetchScalarGridSpec(
        num_scalar_prefetch=2,
        grid=(nblk,),
        in_specs=[
            smem(cur3), smem(nxt3),
            pl.BlockSpec((1, d, dff), wmap),
            pl.BlockSpec((1, 1, dff), wmap),
            pl.BlockSpec((1, d, dff), wmap),
            pl.BlockSpec((1, 1, dff), wmap),
            pl.BlockSpec((1, dff, d), wmap),
            pl.BlockSpec((1, 1, d), wmap),
            pl.BlockSpec(memory_space=pl.ANY),
        ],
        out_specs=pl.BlockSpec((bm * SUB, LANE), lambda i, be, nu: (i, 0)),
        scratch_shapes=[
            pltpu.VMEM((2, bm * SUB, LANE), F32),
            pltpu.VMEM((d, dff), BF16),
            pltpu.VMEM((d, dff), BF16),
            pltpu.VMEM((dff, d), BF16),
            pltpu.SemaphoreType.DMA((2,)),
        ],
    )
    return pl.pallas_call(
        _moe_kernel,
        grid_spec=grid_spec,
        out_shape=jax.ShapeDtypeStruct((nblk * bm * SUB, LANE), F32),
        compiler_params=_cparams(("arbitrary",)),
        name="moe",
    )(blk_e, nused, src, src, wg, bg, wu, bu, wd, bd, hf)


def _final_kernel(pos_ref, posn_ref, x_ref, gt_ref, g_ref, ys_hbm, o_ref, ybuf, gsem, *, nt):
    i = pl.program_id(0)
    tm = x_ref.shape[0]
    cnt = TOP_K * tm

    def step(s):
        @pl.when(i + 1 < nt)
        def _():
            _gather_tiles(posn_ref, ys_hbm, ybuf.at[1 - s], gsem.at[1 - s], cnt)

        pltpu.make_async_copy(ys_hbm.at[pl.ds(0, cnt * SUB)], ybuf.at[s], gsem.at[s]).wait()
        gates = gt_ref[...]
        gk = [jnp.broadcast_to(gates[:, k:k + 1], (tm, LANE)) for k in range(TOP_K)]
        chunks = []
        for j in range(x_ref.shape[1] // LANE):
            parts = [gk[k] * ybuf[s, pl.ds(k * tm * SUB + j, tm, stride=SUB), :] for k in range(TOP_K)]
            chunks.append((parts[0] + parts[1]) + (parts[2] + parts[3]))
        o_ref[...] = _rms(x_ref[...] + jnp.concatenate(chunks, axis=1)) * g_ref[...]

    @pl.when(i == 0)
    def _():
        _gather_tiles(pos_ref, ys_hbm, ybuf.at[0], gsem.at[0], cnt)

    for s in range(2):
        pl.when(lax.rem(i, 2) == s)(functools.partial(step, s))


def _final_call(x2, ys, pos_tiles, gates, g_final, tok0, ntok):
    d = x2.shape[1]
    assert TOP_K == 4 and d == SUB * LANE
    tm = FIN_TM
    assert tok0 % tm == 0 and ntok % tm == 0
    off, nt = tok0 // tm, ntok // tm
    smem = functools.partial(pl.BlockSpec, (1, 1, TOP_K * tm), memory_space=pltpu.SMEM)
    return pl.pallas_call(
        functools.partial(_final_kernel, nt=nt),
        grid=(nt,),
        in_specs=[
            smem(lambda i: (off + i, 0, 0)),
            smem(lambda i: (off + jnp.minimum(i + 1, nt - 1), 0, 0)),
            pl.BlockSpec((tm, d), lambda i: (off + i, 0)),
            pl.BlockSpec((tm, TOP_K), lambda i: (off + i, 0)),
            pl.BlockSpec((1, d), lambda i: (0, 0)),
            pl.BlockSpec(memory_space=pl.ANY),
        ],
        out_specs=pl.BlockSpec((tm, d), lambda i: (i, 0)),
        out_shape=jax.ShapeDtypeStruct((ntok, d), F32),
        scratch_shapes=[pltpu.VMEM((2, TOP_K * tm * SUB, LANE), F32), pltpu.SemaphoreType.DMA((2,))],
        compiler_params=_cparams(("arbitrary",)),
        name="final",
    )(pos_tiles, pos_tiles, x2, gates, g_final, ys)


def _prep_weights(w_in, w_uq, w_ukv, t_max):
    d = w_in.shape[0]
    o = np.cumsum((0, Q_LORA, KV_LORA, ROPE_DIM, NA_WIDTH, NA_WIDTH, NA_WIDTH))
    w_cq, w_ckv, w_kpe = w_in[:, o[0]:o[1]], w_in[:, o[1]:o[2]], w_in[:, o[2]:o[3]]
    w_naq, w_nak, w_nav = w_in[:, o[3]:o[4]], w_in[:, o[4]:o[5]], w_in[:, o[5]:o[6]]
    half = ROPE_DIM // 2
    swap = np.concatenate([np.arange(half, ROPE_DIM), np.arange(half)])
    zpad = lambda w, lo: jnp.pad(w, ((0, 0), (lo, LANE - lo - w.shape[1])))
    win_r = jnp.concatenate([w_naq, w_nak, w_nav, w_cq, w_ckv,
                             zpad(w_kpe, QK_NOPE), zpad(w_kpe[:, swap], QK_NOPE)], axis=1).astype(BF16)

    uq = w_uq.reshape(Q_LORA, MLA_HEADS, QK_HEAD)
    wqa = jnp.pad(uq, ((0, 0), (0, 0), (0, LANE - QK_HEAD))).reshape(Q_LORA, MLA_HEADS * LANE).astype(BF16)
    wqb = jnp.pad(uq[:, :, QK_NOPE:][:, :, swap], ((0, 0), (0, 0), (QK_NOPE, LANE - QK_HEAD)))
    wqb = wqb.reshape(Q_LORA, MLA_HEADS * LANE).astype(BF16)
    ukv = w_ukv.reshape(KV_LORA, MLA_HEADS, QK_NOPE + V_HEAD)
    wuk = jnp.pad(ukv[:, :, :QK_NOPE], ((0, 0), (0, 0), (0, LANE - QK_NOPE)))
    wuk = wuk.reshape(KV_LORA, MLA_HEADS * LANE).astype(BF16)
    wuv = ukv[:, :, QK_NOPE:].reshape(KV_LORA, MLA_WIDTH).astype(BF16)

    pos = jnp.arange(t_max, dtype=F32)
    inv = 1.0 / (ROPE_THETA ** (jnp.arange(0, ROPE_DIM, 2, dtype=F32) / ROPE_DIM))
    ang = pos[:, None] * inv[None, :]
    cos, sin = jnp.cos(ang), jnp.sin(ang)
    c32 = jnp.concatenate([cos, cos], axis=1)
    s32 = jnp.concatenate([-sin, sin], axis=1)
    z = lambda w: jnp.zeros((t_max, w), F32)
    cak = jnp.concatenate([z(QK_NOPE), c32, z(LANE - QK_HEAD)], axis=1)
    sbk = jnp.concatenate([z(QK_NOPE), s32, z(LANE - QK_HEAD)], axis=1)
    scale = QK_HEAD ** -0.5 * LOG2E
    caq = scale * jnp.concatenate([jnp.ones((t_max, QK_NOPE), F32), c32, z(LANE - QK_HEAD)], axis=1)
    tab = jnp.concatenate([caq, scale * sbk, cak, sbk], axis=1)
    del d
    return win_r, wqa, wqb, wuk, wuv, tab


def _route(rec, cnt, bm, tm):
    n = rec.shape[1]
    a = n * TOP_K
    nblk = a // bm + N_EXPERTS
    expert = rec[0:TOP_K].astype(jnp.int32)
    rank = rec[TOP_K:2 * TOP_K].astype(jnp.int32)
    gates = rec[2 * TOP_K:3 * TOP_K].T
    counts = cnt[:, 0].astype(jnp.int32)
    start = jnp.cumsum(counts) - counts
    padded = ((counts + bm - 1) // bm) * bm
    pad_end = jnp.cumsum(padded)
    pad_start = pad_end - padded
    blk0 = jnp.arange(nblk, dtype=jnp.int32) * bm
    blk_e = jnp.minimum(jnp.sum((pad_end[None, :] <= blk0[:, None]).astype(jnp.int32), axis=1), N_EXPERTS - 1)
    nused = (pad_end[-1:] // bm).astype(jnp.int32)
    pos = pad_start[expert] + rank
    pos_tiles = (pos * SUB).reshape(TOP_K, n // tm, tm).transpose(1, 0, 2).reshape(n // tm, 1, TOP_K * tm)
    tok = jnp.broadcast_to(jnp.arange(n, dtype=jnp.int32)[None, :], (TOP_K, n))
    _, tok_sorted = lax.sort((pos.reshape(-1), tok.reshape(-1)), num_keys=1)
    slot_rank = (blk0 - pad_start[blk_e])[:, None] + jnp.arange(bm, dtype=jnp.int32)[None, :]
    valid = slot_rank < counts[blk_e][:, None]
    src = jnp.where(valid, tok_sorted[jnp.clip(start[blk_e][:, None] + slot_rank, 0, a - 1)], 0)
    return (src * SUB)[:, None, :], blk_e.astype(jnp.int32), nused, pos_tiles, gates


def kernel(x_prompt, x_sample, g_attn_norm, w_in, g_q_lora, w_uq, g_kv_lora, w_ukv, na_rpb, g_out_mla,
           g_out_na, w_out, g_ffn_norm, w_router, b_router, w_gate, b_gate, w_up, b_up, w_down, b_down,
           g_final):
    assert g_attn_norm.shape[0] == 1, "single-layer trunk"
    d = x_prompt.shape[-1]
    groups = [x_prompt, x_sample]
    shapes = [(g.shape[0], g.shape[1]) for g in groups]
    for _, t in shapes:
        assert t % TOK_TILE == 0 and t % GRID_W == 0 and (t // GRID_W) % NA_RQ == 0
        assert t // GRID_W >= NA_WROWS + NA_RQ
    xa, xb = (g.reshape(-1, d) for g in groups)
    n = xa.shape[0] + xb.shape[0]
    t_max = max(t for _, t in shapes)

    win_r, wqa, wqb, wuk, wuv, tab = _prep_weights(w_in[0], w_uq[0], w_ukv[0], t_max)
    pos_blocks = [(b * t // TOK_TILE, t // TOK_TILE) for b, t in shapes]
    row = lambda v: v.reshape(1, -1)
    q, k, v, naq, nak, nav = _proj_call(xa, xb, row(g_attn_norm[0]), win_r, row(g_q_lora[0]), wqa, wqb,
                                        row(g_kv_lora[0]), wuk, wuv, tab, pos_blocks)

    bias = _na_bias_table(na_rpb[0])
    a_all = jnp.zeros((n, MLA_WIDTH), BF16)
    n_all = jnp.zeros((n, NA_WIDTH), BF16)
    tok0 = 0
    for b, t in shapes:
        assert tok0 % t == 0 and n % t == 0
        boff = tok0 // t
        view = lambda arr: arr.reshape(n // t, t, arr.shape[-1])
        a_all = _mla_call(view(q), view(k), view(v), b, t, boff, view(a_all)).reshape(n, MLA_WIDTH)
        n_all = _na_call(view(naq), view(nak), view(nav), bias, b, t, boff, view(n_all)).reshape(n, NA_WIDTH)
        tok0 += b * t

    wr = w_router[0].T
    wr_hi = wr.astype(BF16)
    wr_lo = (wr - wr_hi.astype(F32)).astype(BF16)
    b_r = b_router[0].reshape(N_EXPERTS, 1)
    x2, hf, rec, cnt = _mix_call(a_all, n_all, xa, xb, row(g_out_mla[0]), row(g_out_na[0]), w_out[0].astype(BF16),
                                 row(g_ffn_norm[0]), wr_hi, wr_lo, b_r)

    src, blk_e, nused, pos_tiles, gates = _route(rec, cnt, MOE_BM, FIN_TM)
    ys = _moe_call(blk_e, nused, src, w_gate[0], b_gate[0][:, None, :], w_up[0], b_up[0][:, None, :],
                   w_down[0], b_down[0][:, None, :], hf)

    outs, tok0 = [], 0
    for b, t in shapes:
        outs.append(_final_call(x2, ys, pos_tiles, gates, row(g_final), tok0, b * t).reshape(b, t, d))
        tok0 += b * t
    return tuple(outs)
```

```python
import functools

import numpy as np
import jax
import jax.numpy as jnp
from jax import lax
from jax.experimental import pallas as pl
from jax.experimental.pallas import tpu as pltpu

F32 = jnp.float32
BF16 = jnp.bfloat16

EPS = 1e-6
MLA_HEADS = 8
Q_LORA = 256
KV_LORA = 128
QK_NOPE = 64
ROPE_DIM = 32
QK_HEAD = QK_NOPE + ROPE_DIM
V_HEAD = 64
ROPE_THETA = 10000.0
NA_HEADS = 8
NA_HEAD_DIM = 64
NA_WIDTH = NA_HEADS * NA_HEAD_DIM
MLA_WIDTH = MLA_HEADS * V_HEAD
GRID_W = 64
NA_KH = 8
NA_KW = 16
N_EXPERTS = 32
TOP_K = 4
SWIGLU_LIMIT = 7.0
SWIGLU_ALPHA = 1.702

LANE = 128
SUB = 8
VMEM_LIMIT = 56 * 1024 * 1024

TOK_TILE = 512
MLA_TQ = 256
LOG2E = 1.4426950408889634
NA_RQ = 4
NA_WROWS = 12
MOE_BM = 512
FIN_TM = 256
NEG = -1e30

IN_COLS = 3 * NA_WIDTH + Q_LORA + KV_LORA + 2 * LANE


def _rms(x):
    return x * lax.rsqrt(jnp.mean(x * x, axis=-1, keepdims=True) + EPS)


def _dot(a, b):
    return jnp.dot(a, b, preferred_element_type=F32)


def _dot_nt(a, b):
    return lax.dot_general(a, b, (((1,), (1,)), ((), ())), preferred_element_type=F32)


def _cparams(sem):
    return pltpu.CompilerParams(dimension_semantics=sem, vmem_limit_bytes=VMEM_LIMIT)


def _group_tile(xa_ref, xb_ref, n0):
    return jnp.where(pl.program_id(0) < n0, xa_ref[...], xb_ref[...])


def _group_specs(tm, d, n0):
    return [pl.BlockSpec((tm, d), lambda i: (jnp.minimum(i, n0 - 1), 0)),
            pl.BlockSpec((tm, d), lambda i: (jnp.maximum(i - n0, 0), 0))]


def _proj_kernel(xa_ref, xb_ref, g_ref, win_ref, gq_ref, wqa_ref, wqb_ref, gkv_ref, wuk_ref, wuv_ref, tab_ref,
                 q_ref, k_ref, v_ref, naq_ref, nak_ref, nav_ref, *, n0):
    h = (_rms(_group_tile(xa_ref, xb_ref, n0)) * g_ref[...]).astype(BF16)
    proj = _dot(h, win_ref[...])
    o = 0
    naq_ref[...] = (proj[:, o:o + NA_WIDTH] * (NA_HEAD_DIM ** -0.5 * LOG2E)).astype(BF16)
    o += NA_WIDTH
    nak_ref[...] = proj[:, o:o + NA_WIDTH].astype(BF16)
    o += NA_WIDTH
    nav_ref[...] = proj[:, o:o + NA_WIDTH].astype(BF16)
    o += NA_WIDTH
    cq = (_rms(proj[:, o:o + Q_LORA]) * gq_ref[...]).astype(BF16)
    o += Q_LORA
    ckv = (_rms(proj[:, o:o + KV_LORA]) * gkv_ref[...]).astype(BF16)
    o += KV_LORA
    kpe_a = proj[:, o:o + LANE]
    kpe_b = proj[:, o + LANE:o + 2 * LANE]

    tab = tab_ref[...]
    caq, sbq = tab[:, 0:LANE], tab[:, LANE:2 * LANE]
    cak, sbk = tab[:, 2 * LANE:3 * LANE], tab[:, 3 * LANE:4 * LANE]

    qa = _dot(cq, wqa_ref[...])
    qb = _dot(cq, wqb_ref[...])
    kn = _dot(ckv, wuk_ref[...])
    kpe = kpe_a * cak + kpe_b * sbk
    for hd in range(MLA_HEADS):
        sl = slice(hd * LANE, (hd + 1) * LANE)
        q_ref[:, sl] = (qa[:, sl] * caq + qb[:, sl] * sbq).astype(BF16)
        k_ref[:, sl] = (kn[:, sl] + kpe).astype(BF16)
    v_ref[...] = _dot(ckv, wuv_ref[...]).astype(BF16)


def _proj_call(xa, xb, g_attn, win_r, g_q, wqa, wqb, g_kv, wuk, wuv, tab, pos_blocks):
    d = xa.shape[1]
    n = xa.shape[0] + xb.shape[0]
    tm = TOK_TILE
    (n0, pb0), (_, pb1) = pos_blocks

    def tab_map(i):
        return (jnp.where(i < n0, i % pb0, (i - n0) % pb1), 0)

    row = lambda i: (i, 0)
    fix = lambda i: (0, 0)
    out_w = (MLA_HEADS * LANE, MLA_HEADS * LANE, MLA_WIDTH, NA_WIDTH, NA_WIDTH, NA_WIDTH)
    return pl.pallas_call(
        functools.partial(_proj_kernel, n0=n0),
        grid=(n // tm,),
        in_specs=_group_specs(tm, d, n0) + [
            pl.BlockSpec((1, d), fix),
            pl.BlockSpec(win_r.shape, fix),
            pl.BlockSpec((1, Q_LORA), fix),
            pl.BlockSpec(wqa.shape, fix),
            pl.BlockSpec(wqb.shape, fix),
            pl.BlockSpec((1, KV_LORA), fix),
            pl.BlockSpec(wuk.shape, fix),
            pl.BlockSpec(wuv.shape, fix),
            pl.BlockSpec((tm, 4 * LANE), tab_map),
        ],
        out_specs=[pl.BlockSpec((tm, w), row) for w in out_w],
        out_shape=[jax.ShapeDtypeStruct((n, w), BF16) for w in out_w],
        compiler_params=_cparams(("parallel",)),
        name="proj",
    )(xa, xb, g_attn, win_r, g_q, wqa, wqb, g_kv, wuk, wuv, tab)


def _softmax_pv(s, v_ones):
    p = jnp.exp2(s - jnp.max(s, axis=-1, keepdims=True))
    acc = _dot(p.astype(BF16), v_ones)
    return acc[:, :LANE] / acc[:, LANE:]


def _mla_kernel(q_ref, k_ref, v_ref, prev_ref, o_ref):
    del prev_ref
    tq, t = q_ref.shape[1], k_ref.shape[1]
    low = lax.broadcasted_iota(jnp.int32, (tq, LANE), 1) < V_HEAD
    ones = jnp.ones((t, LANE), BF16)
    for pp in range(MLA_HEADS // 2):
        v_ones = jnp.concatenate([v_ref[0, :, pp * LANE:(pp + 1) * LANE], ones], axis=1)
        res = []
        for h in (2 * pp, 2 * pp + 1):
            s = _dot_nt(q_ref[0, :, h * LANE:(h + 1) * LANE], k_ref[0, :, h * LANE:(h + 1) * LANE])
            res.append(_softmax_pv(s, v_ones))
        o_ref[0, :, pp * LANE:(pp + 1) * LANE] = jnp.where(low, res[0], res[1]).astype(o_ref.dtype)


def _mla_call(q, k, v, nb, t, boff, prev):
    tq = min(MLA_TQ, t)
    return pl.pallas_call(
        _mla_kernel,
        grid=(nb, t // tq),
        in_specs=[
            pl.BlockSpec((1, tq, MLA_HEADS * LANE), lambda b, i: (b + boff, i, 0)),
            pl.BlockSpec((1, t, MLA_HEADS * LANE), lambda b, i: (b + boff, 0, 0)),
            pl.BlockSpec((1, t, MLA_WIDTH), lambda b, i: (b + boff, 0, 0)),
            pl.BlockSpec(memory_space=pl.ANY),
        ],
        out_specs=pl.BlockSpec((1, tq, MLA_WIDTH), lambda b, i: (b + boff, i, 0)),
        out_shape=jax.ShapeDtypeStruct(prev.shape, prev.dtype),
        input_output_aliases={3: 0},
        compiler_params=_cparams(("parallel", "arbitrary")),
        name="mla",
    )(q, k, v, prev)


def _na_base(i, rows):
    return jnp.clip(i * NA_RQ - NA_KH // 2, 0, rows - NA_WROWS)


def _na_kernel(q_ref, k_ref, v_ref, b_ref, prev_ref, o_ref):
    del prev_ref
    nq = q_ref.shape[1]
    rows = k_ref.shape[1] // GRID_W
    nk = NA_WROWS * GRID_W
    off = pl.multiple_of(_na_base(pl.program_id(1), rows) * GRID_W, GRID_W)
    lane = lax.broadcasted_iota(jnp.int32, (nq, LANE), 1)
    low = lane < NA_HEAD_DIM
    ones = jnp.ones((nk, LANE), BF16)
    for p in range(NA_HEADS // 2):
        sl = slice(p * LANE, (p + 1) * LANE)
        q2 = q_ref[0, :, sl]
        kw = k_ref[0, pl.ds(off, nk), sl]
        v_ones = jnp.concatenate([v_ref[0, pl.ds(off, nk), sl], ones], axis=1)
        outs = []
        for hh in range(2):
            qm = jnp.where(low, q2, 0) if hh == 0 else jnp.where(low, 0, q2)
            s = _dot_nt(qm, kw) + b_ref[0, 2 * p + hh]
            outs.append(_softmax_pv(s, v_ones))
        o_ref[0, :, sl] = jnp.where(low, outs[0], outs[1]).astype(o_ref.dtype)


def _na_call(q, k, v, bias, nb, t, boff, prev):
    rows = t // GRID_W
    nq = NA_RQ * GRID_W
    nk = NA_WROWS * GRID_W

    def bias_map(b, i):
        return ((i * NA_RQ - _na_base(i, rows)) // (NA_KH // 2), 0, 0, 0)

    return pl.pallas_call(
        _na_kernel,
        grid=(nb, rows // NA_RQ),
        in_specs=[
            pl.BlockSpec((1, nq, NA_WIDTH), lambda b, i: (b + boff, i, 0)),
            pl.BlockSpec((1, t, NA_WIDTH), lambda b, i: (b + boff, 0, 0)),
            pl.BlockSpec((1, t, NA_WIDTH), lambda b, i: (b + boff, 0, 0)),
            pl.BlockSpec((1, NA_HEADS, nq, nk), bias_map),
            pl.BlockSpec(memory_space=pl.ANY),
        ],
        out_specs=pl.BlockSpec((1, nq, NA_WIDTH), lambda b, i: (b + boff, i, 0)),
        out_shape=jax.ShapeDtypeStruct(prev.shape, prev.dtype),
        input_output_aliases={4: 0},
        compiler_params=_cparams(("parallel", "arbitrary")),
        name="natten",
    )(q, k, v, bias, prev)


def _na_bias_table(rpb):
    rows = 32
    c = np.arange(GRID_W)
    cstart = np.clip(c - NA_KW // 2, 0, GRID_W - NA_KW)
    col_ok = (c[None, :] >= cstart[:, None]) & (c[None, :] < cstart[:, None] + NA_KW)
    cidx = np.clip(c[None, :] - c[:, None] + NA_KW - 1, 0, 2 * NA_KW - 2)
    onehot = (cidx.reshape(-1)[None, :] == np.arange(2 * NA_KW - 1)[:, None]).astype(np.float32)
    tabs = []
    for i0 in (0, NA_RQ, rows - NA_RQ):
        base = int(np.clip(i0 - NA_KH // 2, 0, rows - NA_WROWS))
        qi = i0 + np.arange(NA_RQ)
        start = np.clip(qi - NA_KH // 2, 0, rows - NA_KH)
        kr = base + np.arange(NA_WROWS)
        row_ok = (kr[None, :] >= start[:, None]) & (kr[None, :] < start[:, None] + NA_KH)
        ridx = np.clip(kr[None, :] - qi[:, None] + NA_KH - 1, 0, 2 * NA_KH - 2)
        by_row = rpb[:, ridx, :].astype(F32)
        vals = jnp.einsum("hjrx,xq->hjrq", by_row, onehot, precision=lax.Precision.HIGHEST)
        vals = vals.reshape(NA_HEADS, NA_RQ, NA_WROWS, GRID_W, GRID_W).transpose(0, 1, 3, 2, 4)
        ok = row_ok[:, None, :, None] & col_ok[None, :, None, :]
        tab = jnp.where(ok[None], vals * LOG2E, NEG)
        tabs.append(tab.reshape(NA_HEADS, NA_RQ * GRID_W, NA_WROWS * GRID_W))
    return jnp.stack(tabs)


REC_ROWS = 16


def _route_tile(logits, tri, run):
    ne, tm = logits.shape
    row = lax.broadcasted_iota(jnp.int32, (ne, tm), 0)
    left = logits
    onehot = jnp.zeros((ne, tm), F32)
    vals, idxs = [], []
    for _ in range(TOP_K):
        m = jnp.max(left, axis=0, keepdims=True)
        idx = jnp.min(jnp.where(left == m, row, ne), axis=0, keepdims=True)
        sel = row == idx
        onehot = onehot + sel.astype(F32)
        left = jnp.where(sel, NEG, left)
        vals.append(m)
        idxs.append(idx)
    ex = [jnp.exp(v - vals[0]) for v in vals]
    den = (ex[0] + ex[1]) + (ex[2] + ex[3])
    before = _dot(onehot.astype(BF16), tri) + run
    ranks = [jnp.sum(jnp.where(row == idx, before, 0.0), axis=0, keepdims=True) for idx in idxs]
    rec = [idx.astype(F32) for idx in idxs] + ranks + [e / den for e in ex]
    rec.append(jnp.zeros((REC_ROWS - len(rec), tm), F32))
    return jnp.concatenate(rec, axis=0), jnp.sum(onehot, axis=1, keepdims=True)


def _mix_kernel(a_ref, n_ref, xa_ref, xb_ref, gm_ref, gn_ref, wo_ref, gf_ref, wrh_ref, wrl_ref, br_ref, tri_ref,
                x2_ref, hf_ref, rt_ref, cnt_ref, *, n0):
    @pl.when(pl.program_id(0) == 0)
    def _():
        cnt_ref[...] = jnp.zeros(cnt_ref.shape, F32)

    an = _rms(a_ref[...].astype(F32)) * gm_ref[...]
    nn = _rms(n_ref[...].astype(F32)) * gn_ref[...]
    cat = jnp.concatenate([an, nn], axis=-1).astype(BF16)
    x2 = _group_tile(xa_ref, xb_ref, n0) + _dot(cat, wo_ref[...])
    x2_ref[...] = x2
    hf = _rms(x2) * gf_ref[...]
    tm = hf.shape[0]
    for j in range(hf.shape[1] // LANE):
        hf_ref[pl.ds(j, tm, stride=SUB), :] = hf[:, j * LANE:(j + 1) * LANE]
    hi = hf.astype(BF16)
    lo = (hf - hi.astype(F32)).astype(BF16)
    logits = (_dot_nt(wrh_ref[...], hi) + (_dot_nt(wrh_ref[...], lo) + _dot_nt(wrl_ref[...], hi))) + br_ref[...]
    rt_ref[...], tile_cnt = _route_tile(logits, tri_ref[...], cnt_ref[:, 0:1])
    cnt_ref[...] += tile_cnt


def _mix_call(a, nat, xa, xb, g_mla, g_na, w_out, g_ffn, wr_hi, wr_lo, b_r):
    d = xa.shape[1]
    n = xa.shape[0] + xb.shape[0]
    tm = TOK_TILE
    n0 = xa.shape[0] // tm
    row = lambda i: (i, 0)
    fix = lambda i: (0, 0)
    tri = jnp.asarray(np.triu(np.ones((tm, tm), np.float32), 1), BF16)
    return pl.pallas_call(
        functools.partial(_mix_kernel, n0=n0),
        grid=(n // tm,),
        in_specs=[pl.BlockSpec((tm, MLA_WIDTH), row), pl.BlockSpec((tm, NA_WIDTH), row)] + _group_specs(tm, d, n0) + [
            pl.BlockSpec((1, MLA_WIDTH), fix),
            pl.BlockSpec((1, NA_WIDTH), fix),
            pl.BlockSpec(w_out.shape, fix),
            pl.BlockSpec((1, d), fix),
            pl.BlockSpec(wr_hi.shape, fix),
            pl.BlockSpec(wr_lo.shape, fix),
            pl.BlockSpec((N_EXPERTS, 1), fix),
            pl.BlockSpec((tm, tm), fix),
        ],
        out_specs=[pl.BlockSpec((tm, d), row), pl.BlockSpec((tm * SUB, LANE), row),
                   pl.BlockSpec((REC_ROWS, tm), lambda i: (0, i)), pl.BlockSpec((N_EXPERTS, LANE), fix)],
        out_shape=[jax.ShapeDtypeStruct((n, d), F32), jax.ShapeDtypeStruct((n * SUB, LANE), F32),
                   jax.ShapeDtypeStruct((REC_ROWS, n), F32), jax.ShapeDtypeStruct((N_EXPERTS, LANE), F32)],
        compiler_params=_cparams(("arbitrary",)),
        name="mix",
    )(a, nat, xa, xb, g_mla, g_na, w_out, g_ffn, wr_hi, wr_lo, b_r, tri)


def _gather_tiles(idx_ref, src_hbm, buf, sem, count):
    for r in range(count):
        src = src_hbm.at[pl.ds(pl.multiple_of(idx_ref[0, 0, r], SUB), SUB)]
        pltpu.make_async_copy(src, buf.at[pl.ds(r * SUB, SUB)], sem).start(priority=r % 2)


def _moe_kernel(be_ref, nu_ref, src_ref, srcn_ref, wgf_ref, bg_ref, wuf_ref, bu_ref, wdf_ref, bd_ref, hf_hbm,
                y_ref, xbuf, wg_ref, wu_ref, wd_ref, gsem):
    i = pl.program_id(0)
    nu = nu_ref[0]
    bm = xbuf.shape[1] // SUB
    nchunk = wd_ref.shape[1] // LANE
    blk = pl.ds(0, bm * SUB)

    def step(s):
        @pl.when(i + 1 < nu)
        def _():
            _gather_tiles(srcn_ref, hf_hbm, xbuf.at[1 - s], gsem.at[1 - s], bm)

        @pl.when(i >= nu)
        def _():
            y_ref[...] = jnp.zeros(y_ref.shape, F32)

        @pl.when(i < nu)
        def _():
            pltpu.make_async_copy(hf_hbm.at[blk], xbuf.at[s], gsem.at[s]).wait()
            x = jnp.concatenate([xbuf[s, pl.ds(j, bm, stride=SUB), :] for j in range(nchunk)], axis=1).astype(BF16)
            g = _dot(x, wg_ref[...]) + bg_ref[0]
            u = _dot(x, wu_ref[...]) + bu_ref[0]
            g = jnp.minimum(g, SWIGLU_LIMIT)
            u = jnp.clip(u, -SWIGLU_LIMIT, SWIGLU_LIMIT)
            h = (u + 1.0) * (g * jax.nn.sigmoid(SWIGLU_ALPHA * g))
            y = _dot(h.astype(BF16), wd_ref[...]) + bd_ref[0]
            for j in range(nchunk):
                y_ref[pl.ds(j, bm, stride=SUB), :] = y[:, j * LANE:(j + 1) * LANE]

    @pl.when(i == 0)
    def _():
        _gather_tiles(src_ref, hf_hbm, xbuf.at[0], gsem.at[0], bm)

    blk_i = jnp.minimum(i, nu - 1)

    @pl.when((i == 0) | ((i < nu) & (be_ref[blk_i] != be_ref[jnp.maximum(blk_i - 1, 0)])))
    def _():
        wg_ref[...] = wgf_ref[0].astype(BF16)
        wu_ref[...] = wuf_ref[0].astype(BF16)
        wd_ref[...] = wdf_ref[0].astype(BF16)

    for s in range(2):
        pl.when(lax.rem(i, 2) == s)(functools.partial(step, s))


def _moe_call(blk_e, nused, src, wg, bg, wu, bu, wd, bd, hf):
    nblk, _, bm = src.shape
    d = wg.shape[1]
    dff = wg.shape[2]
    assert d == SUB * LANE and hf.shape[1] == LANE
    last = lambda i, nu: jnp.minimum(i, nu[0] - 1)
    cur3 = lambda i, be, nu: (last(i, nu), 0, 0)
    nxt3 = lambda i, be, nu: (last(i + 1, nu), 0, 0)
    wmap = lambda i, be, nu: (be[last(i, nu)], 0, 0)
    smem = functools.partial(pl.BlockSpec, (1, 1, bm), memory_space=pltpu.SMEM)
    grid_spec = pltpu.PrefetchScalarGridSpec(
        num_scalar_prefetch=2,
        grid=(nblk,),
        in_specs=[
            smem(cur3), smem(nxt3),
            pl.BlockSpec((1, d, dff), wmap),
            pl.BlockSpec((1, 1, dff), wmap),
            pl.BlockSpec((1, d, dff), wmap),
            pl.BlockSpec((1, 1, dff), wmap),
            pl.BlockSpec((1, dff, d), wmap),
            pl.BlockSpec((1, 1, d), wmap),
            pl.BlockSpec(memory_space=pl.ANY),
        ],
        out_specs=pl.BlockSpec((bm * SUB, LANE), lambda i, be, nu: (i, 0)),
        scratch_shapes=[
            pltpu.VMEM((2, bm * SUB, LANE), F32),
            pltpu.VMEM((d, dff), BF16),
            pltpu.VMEM((d, dff), BF16),
            pltpu.VMEM((dff, d), BF16),
            pltpu.SemaphoreType.DMA((2,)),
        ],
    )
    return pl.pallas_call(
        _moe_kernel,
        grid_spec=grid_spec,
        out_shape=jax.ShapeDtypeStruct((nblk * bm * SUB, LANE), F32),
        compiler_params=_cparams(("arbitrary",)),
        name="moe",
    )(blk_e, nused, src, src, wg, bg, wu, bu, wd, bd, hf)


def _final_kernel(pos_ref, posn_ref, x_ref, gt_ref, g_ref, ys_hbm, o_ref, ybuf, gsem, *, nt):
    i = pl.program_id(0)
    tm = x_ref.shape[0]
    cnt = TOP_K * tm

    def step(s):
        @pl.when(i + 1 < nt)
        def _():
            _gather_tiles(posn_ref, ys_hbm, ybuf.at[1 - s], gsem.at[1 - s], cnt)

        pltpu.make_async_copy(ys_hbm.at[pl.ds(0, cnt * SUB)], ybuf.at[s], gsem.at[s]).wait()
        gates = gt_ref[...]
        gk = [jnp.broadcast_to(gates[:, k:k + 1], (tm, LANE)) for k in range(TOP_K)]
        chunks = []
        for j in range(x_ref.shape[1] // LANE):
            parts = [gk[k] * ybuf[s, pl.ds(k * tm * SUB + j, tm, stride=SUB), :] for k in range(TOP_K)]
            chunks.append((parts[0] + parts[1]) + (parts[2] + parts[3]))
        o_ref[...] = _rms(x_ref[...] + jnp.concatenate(chunks, axis=1)) * g_ref[...]

    @pl.when(i == 0)
    def _():
        _gather_tiles(pos_ref, ys_hbm, ybuf.at[0], gsem.at[0], cnt)

    for s in range(2):
        pl.when(lax.rem(i, 2) == s)(functools.partial(step, s))


def _final_call(x2, ys, pos_tiles, gates, g_final, tok0, ntok):
    d = x2.shape[1]
    assert TOP_K == 4 and d == SUB * LANE
    tm = FIN_TM
    assert tok0 % tm == 0 and ntok % tm == 0
    off, nt = tok0 // tm, ntok // tm
    smem = functools.partial(pl.BlockSpec, (1, 1, TOP_K * tm), memory_space=pltpu.SMEM)
    return pl.pallas_call(
        functools.partial(_final_kernel, nt=nt),
        grid=(nt,),
        in_specs=[
            smem(lambda i: (off + i, 0, 0)),
            smem(lambda i: (off + jnp.minimum(i + 1, nt - 1), 0, 0)),
            pl.BlockSpec((tm, d), lambda i: (off + i, 0)),
            pl.BlockSpec((tm, TOP_K), lambda i: (off + i, 0)),
            pl.BlockSpec((1, d), lambda i: (0, 0)),
            pl.BlockSpec(memory_space=pl.ANY),
        ],
        out_specs=pl.BlockSpec((tm, d), lambda i: (i, 0)),
        out_shape=jax.ShapeDtypeStruct((ntok, d), F32),
        scratch_shapes=[pltpu.VMEM((2, TOP_K * tm * SUB, LANE), F32), pltpu.SemaphoreType.DMA((2,))],
        compiler_params=_cparams(("arbitrary",)),
        name="final",
    )(pos_tiles, pos_tiles, x2, gates, g_final, ys)


def _prep_weights(w_in, w_uq, w_ukv, t_max):
    d = w_in.shape[0]
    o = np.cumsum((0, Q_LORA, KV_LORA, ROPE_DIM, NA_WIDTH, NA_WIDTH, NA_WIDTH))
    w_cq, w_ckv, w_kpe = w_in[:, o[0]:o[1]], w_in[:, o[1]:o[2]], w_in[:, o[2]:o[3]]
    w_naq, w_nak, w_nav = w_in[:, o[3]:o[4]], w_in[:, o[4]:o[5]], w_in[:, o[5]:o[6]]
    half = ROPE_DIM // 2
    swap = np.concatenate([np.arange(half, ROPE_DIM), np.arange(half)])
    zpad = lambda w, lo: jnp.pad(w, ((0, 0), (lo, LANE - lo - w.shape[1])))
    win_r = jnp.concatenate([w_naq, w_nak, w_nav, w_cq, w_ckv,
                             zpad(w_kpe, QK_NOPE), zpad(w_kpe[:, swap], QK_NOPE)], axis=1).astype(BF16)

    uq = w_uq.reshape(Q_LORA, MLA_HEADS, QK_HEAD)
    wqa = jnp.pad(uq, ((0, 0), (0, 0), (0, LANE - QK_HEAD))).reshape(Q_LORA, MLA_HEADS * LANE).astype(BF16)
    wqb = jnp.pad(uq[:, :, QK_NOPE:][:, :, swap], ((0, 0), (0, 0), (QK_NOPE, LANE - QK_HEAD)))
    wqb = wqb.reshape(Q_LORA, MLA_HEADS * LANE).astype(BF16)
    ukv = w_ukv.reshape(KV_LORA, MLA_HEADS, QK_NOPE + V_HEAD)
    wuk = jnp.pad(ukv[:, :, :QK_NOPE], ((0, 0), (0, 0), (0, LANE - QK_NOPE)))
    wuk = wuk.reshape(KV_LORA, MLA_HEADS * LANE).astype(BF16)
    wuv = ukv[:, :, QK_NOPE:].reshape(KV_LORA, MLA_WIDTH).astype(BF16)

    pos = jnp.arange(t_max, dtype=F32)
    inv = 1.0 / (ROPE_THETA ** (jnp.arange(0, ROPE_DIM, 2, dtype=F32) / ROPE_DIM))
    ang = pos[:, None] * inv[None, :]
    cos, sin = jnp.cos(ang), jnp.sin(ang)
    c32 = jnp.concatenate([cos, cos], axis=1)
    s32 = jnp.concatenate([-sin, sin], axis=1)
    z = lambda w: jnp.zeros((t_max, w), F32)
    cak = jnp.concatenate([z(QK_NOPE), c32, z(LANE - QK_HEAD)], axis=1)
    sbk = jnp.concatenate([z(QK_NOPE), s32, z(LANE - QK_HEAD)], axis=1)
    scale = QK_HEAD ** -0.5 * LOG2E
    caq = scale * jnp.concatenate([jnp.ones((t_max, QK_NOPE), F32), c32, z(LANE - QK_HEAD)], axis=1)
    tab = jnp.concatenate([caq, scale * sbk, cak, sbk], axis=1)
    del d
    return win_r, wqa, wqb, wuk, wuv, tab


def _route(rec, cnt, bm, tm):
    n = rec.shape[1]
    a = n * TOP_K
    nblk = a // bm + N_EXPERTS
    expert = rec[0:TOP_K].astype(jnp.int32)
    rank = rec[TOP_K:2 * TOP_K].astype(jnp.int32)
    gates = rec[2 * TOP_K:3 * TOP_K].T
    counts = cnt[:, 0].astype(jnp.int32)
    start = jnp.cumsum(counts) - counts
    padded = ((counts + bm - 1) // bm) * bm
    pad_end = jnp.cumsum(padded)
    pad_start = pad_end - padded
    blk0 = jnp.arange(nblk, dtype=jnp.int32) * bm
    blk_e = jnp.minimum(jnp.sum((pad_end[None, :] <= blk0[:, None]).astype(jnp.int32), axis=1), N_EXPERTS - 1)
    nused = (pad_end[-1:] // bm).astype(jnp.int32)
    experts = jnp.arange(N_EXPERTS, dtype=jnp.int32)[:, None, None]
    seg = jnp.sum(jnp.where(expert[None] == experts, pad_start[:, None, None], 0), axis=0)
    pos = seg + rank
    pos_tiles = (pos * SUB).reshape(TOP_K, n // tm, tm).transpose(1, 0, 2).reshape(n // tm, 1, TOP_K * tm)
    tok = jnp.broadcast_to(jnp.arange(n, dtype=jnp.int32)[None, :], (TOP_K, n))
    _, tok_sorted = lax.sort((pos.reshape(-1), tok.reshape(-1)), num_keys=1)
    slot_rank = (blk0 - pad_start[blk_e])[:, None] + jnp.arange(bm, dtype=jnp.int32)[None, :]
    valid = slot_rank < counts[blk_e][:, None]
    src = jnp.where(valid, tok_sorted[jnp.clip(start[blk_e][:, None] + slot_rank, 0, a - 1)], 0)
    return (src * SUB)[:, None, :], blk_e.astype(jnp.int32), nused, pos_tiles, gates


def kernel(x_prompt, x_sample, g_attn_norm, w_in, g_q_lora, w_uq, g_kv_lora, w_ukv, na_rpb, g_out_mla,
           g_out_na, w_out, g_ffn_norm, w_router, b_router, w_gate, b_gate, w_up, b_up, w_down, b_down,
           g_final):
    assert g_attn_norm.shape[0] == 1, "single-layer trunk"
    d = x_prompt.shape[-1]
    groups = [x_prompt, x_sample]
    shapes = [(g.shape[0], g.shape[1]) for g in groups]
    for _, t in shapes:
        assert t % TOK_TILE == 0 and t % GRID_W == 0 and (t // GRID_W) % NA_RQ == 0
        assert t // GRID_W >= NA_WROWS + NA_RQ
    xa, xb = (g.reshape(-1, d) for g in groups)
    n = xa.shape[0] + xb.shape[0]
    t_max = max(t for _, t in shapes)

    win_r, wqa, wqb, wuk, wuv, tab = _prep_weights(w_in[0], w_uq[0], w_ukv[0], t_max)
    pos_blocks = [(b * t // TOK_TILE, t // TOK_TILE) for b, t in shapes]
    row = lambda v: v.reshape(1, -1)
    q, k, v, naq, nak, nav = _proj_call(xa, xb, row(g_attn_norm[0]), win_r, row(g_q_lora[0]), wqa, wqb,
                                        row(g_kv_lora[0]), wuk, wuv, tab, pos_blocks)

    bias = _na_bias_table(na_rpb[0])
    a_all = jnp.zeros((n, MLA_WIDTH), BF16)
    n_all = jnp.zeros((n, NA_WIDTH), BF16)
    tok0 = 0
    for b, t in shapes:
        assert tok0 % t == 0 and n % t == 0
        boff = tok0 // t
        view = lambda arr: arr.reshape(n // t, t, arr.shape[-1])
        a_all = _mla_call(view(q), view(k), view(v), b, t, boff, view(a_all)).reshape(n, MLA_WIDTH)
        n_all = _na_call(view(naq), view(nak), view(nav), bias, b, t, boff, view(n_all)).reshape(n, NA_WIDTH)
        tok0 += b * t

    wr = w_router[0].T
    wr_hi = wr.astype(BF16)
    wr_lo = (wr - wr_hi.astype(F32)).astype(BF16)
    b_r = b_router[0].reshape(N_EXPERTS, 1)
    x2, hf, rec, cnt = _mix_call(a_all, n_all, xa, xb, row(g_out_mla[0]), row(g_out_na[0]), w_out[0].astype(BF16),
                                 row(g_ffn_norm[0]), wr_hi, wr_lo, b_r)

    src, blk_e, nused, pos_tiles, gates = _route(rec, cnt, MOE_BM, FIN_TM)
    ys = _moe_call(blk_e, nused, src, w_gate[0], b_gate[0][:, None, :], w_up[0], b_up[0][:, None, :],
                   w_down[0], b_down[0][:, None, :], hf)

    outs, tok0 = [], 0
    for b, t in shapes:
        outs.append(_final_call(x2, ys, pos_tiles, gates, row(g_final), tok0, b * t).reshape(b, t, d))
        tok0 += b * t
    return tuple(outs)
```

```python
import functools

import numpy as np
import jax
import jax.numpy as jnp
from jax import lax
from jax.experimental import pallas as pl
from jax.experimental.pallas import tpu as pltpu

F32 = jnp.float32
BF16 = jnp.bfloat16

EPS = 1e-6
MLA_HEADS = 8
Q_LORA = 256
KV_LORA = 128
QK_NOPE = 64
ROPE_DIM = 32
QK_HEAD = QK_NOPE + ROPE_DIM
V_HEAD = 64
ROPE_THETA = 10000.0
NA_HEADS = 8
NA_HEAD_DIM = 64
NA_WIDTH = NA_HEADS * NA_HEAD_DIM
MLA_WIDTH = MLA_HEADS * V_HEAD
GRID_W = 64
NA_KH = 8
NA_KW = 16
N_EXPERTS = 32
TOP_K = 4
SWIGLU_LIMIT = 7.0
SWIGLU_ALPHA = 1.702

LANE = 128
SUB = 8
VMEM_LIMIT = 56 * 1024 * 1024

TOK_TILE = 512
MLA_TQ = 256
LOG2E = 1.4426950408889634
NA_RQ = 4
NA_WROWS = 12
MOE_BM = 512
MOE_PIECES = 4
FIN_TM = 256
NEG = -1e30

IN_COLS = 3 * NA_WIDTH + Q_LORA + KV_LORA + 2 * LANE


def _rms(x):
    return x * lax.rsqrt(jnp.mean(x * x, axis=-1, keepdims=True) + EPS)


def _dot(a, b):
    return jnp.dot(a, b, preferred_element_type=F32)


def _dot_nt(a, b):
    return lax.dot_general(a, b, (((1,), (1,)), ((), ())), preferred_element_type=F32)


def _cparams(sem):
    return pltpu.CompilerParams(dimension_semantics=sem, vmem_limit_bytes=VMEM_LIMIT)


def _group_tile(xa_ref, xb_ref, n0):
    return jnp.where(pl.program_id(0) < n0, xa_ref[...], xb_ref[...])


def _group_specs(tm, d, n0):
    return [pl.BlockSpec((tm, d), lambda i: (jnp.minimum(i, n0 - 1), 0)),
            pl.BlockSpec((tm, d), lambda i: (jnp.maximum(i - n0, 0), 0))]


def _proj_kernel(xa_ref, xb_ref, g_ref, win_ref, gq_ref, wqa_ref, wqb_ref, gkv_ref, wuk_ref, wuv_ref, tab_ref,
                 q_ref, k_ref, v_ref, naq_ref, nak_ref, nav_ref, *, n0):
    h = (_rms(_group_tile(xa_ref, xb_ref, n0)) * g_ref[...]).astype(BF16)
    proj = _dot(h, win_ref[...])
    o = 0
    naq_ref[...] = (proj[:, o:o + NA_WIDTH] * (NA_HEAD_DIM ** -0.5 * LOG2E)).astype(BF16)
    o += NA_WIDTH
    nak_ref[...] = proj[:, o:o + NA_WIDTH].astype(BF16)
    o += NA_WIDTH
    nav_ref[...] = proj[:, o:o + NA_WIDTH].astype(BF16)
    o += NA_WIDTH
    cq = (_rms(proj[:, o:o + Q_LORA]) * gq_ref[...]).astype(BF16)
    o += Q_LORA
    ckv = (_rms(proj[:, o:o + KV_LORA]) * gkv_ref[...]).astype(BF16)
    o += KV_LORA
    kpe_a = proj[:, o:o + LANE]
    kpe_b = proj[:, o + LANE:o + 2 * LANE]

    tab = tab_ref[...]
    caq, sbq = tab[:, 0:LANE], tab[:, LANE:2 * LANE]
    cak, sbk = tab[:, 2 * LANE:3 * LANE], tab[:, 3 * LANE:4 * LANE]

    qa = _dot(cq, wqa_ref[...])
    qb = _dot(cq, wqb_ref[...])
    kn = _dot(ckv, wuk_ref[...])
    kpe = kpe_a * cak + kpe_b * sbk
    for hd in range(MLA_HEADS):
        sl = slice(hd * LANE, (hd + 1) * LANE)
        q_ref[:, sl] = (qa[:, sl] * caq + qb[:, sl] * sbq).astype(BF16)
        k_ref[:, sl] = (kn[:, sl] + kpe).astype(BF16)
    v_ref[...] = _dot(ckv, wuv_ref[...]).astype(BF16)


def _proj_call(xa, xb, g_attn, win_r, g_q, wqa, wqb, g_kv, wuk, wuv, tab, pos_blocks):
    d = xa.shape[1]
    n = xa.shape[0] + xb.shape[0]
    tm = TOK_TILE
    (n0, pb0), (_, pb1) = pos_blocks

    def tab_map(i):
        return (jnp.where(i < n0, i % pb0, (i - n0) % pb1), 0)

    row = lambda i: (i, 0)
    fix = lambda i: (0, 0)
    out_w = (MLA_HEADS * LANE, MLA_HEADS * LANE, MLA_WIDTH, NA_WIDTH, NA_WIDTH, NA_WIDTH)
    return pl.pallas_call(
        functools.partial(_proj_kernel, n0=n0),
        grid=(n // tm,),
        in_specs=_group_specs(tm, d, n0) + [
            pl.BlockSpec((1, d), fix),
            pl.BlockSpec(win_r.shape, fix),
            pl.BlockSpec((1, Q_LORA), fix),
            pl.BlockSpec(wqa.shape, fix),
            pl.BlockSpec(wqb.shape, fix),
            pl.BlockSpec((1, KV_LORA), fix),
            pl.BlockSpec(wuk.shape, fix),
            pl.BlockSpec(wuv.shape, fix),
            pl.BlockSpec((tm, 4 * LANE), tab_map),
        ],
        out_specs=[pl.BlockSpec((tm, w), row) for w in out_w],
        out_shape=[jax.ShapeDtypeStruct((n, w), BF16) for w in out_w],
        compiler_params=_cparams(("parallel",)),
        name="proj",
    )(xa, xb, g_attn, win_r, g_q, wqa, wqb, g_kv, wuk, wuv, tab)


def _softmax_pv(s, v_ones):
    p = jnp.exp2(s - jnp.max(s, axis=-1, keepdims=True))
    acc = _dot(p.astype(BF16), v_ones)
    return acc[:, :LANE] / acc[:, LANE:]


def _mla_kernel(q_ref, k_ref, v_ref, prev_ref, o_ref):
    del prev_ref
    tq, t = q_ref.shape[1], k_ref.shape[1]
    low = lax.broadcasted_iota(jnp.int32, (tq, LANE), 1) < V_HEAD
    ones = jnp.ones((t, LANE), BF16)
    for pp in range(MLA_HEADS // 2):
        v_ones = jnp.concatenate([v_ref[0, :, pp * LANE:(pp + 1) * LANE], ones], axis=1)
        res = []
        for h in (2 * pp, 2 * pp + 1):
            s = _dot_nt(q_ref[0, :, h * LANE:(h + 1) * LANE], k_ref[0, :, h * LANE:(h + 1) * LANE])
            res.append(_softmax_pv(s, v_ones))
        o_ref[0, :, pp * LANE:(pp + 1) * LANE] = jnp.where(low, res[0], res[1]).astype(o_ref.dtype)


def _mla_call(q, k, v, nb, t, boff, prev):
    tq = min(MLA_TQ, t)
    return pl.pallas_call(
        _mla_kernel,
        grid=(nb, t // tq),
        in_specs=[
            pl.BlockSpec((1, tq, MLA_HEADS * LANE), lambda b, i: (b + boff, i, 0)),
            pl.BlockSpec((1, t, MLA_HEADS * LANE), lambda b, i: (b + boff, 0, 0)),
            pl.BlockSpec((1, t, MLA_WIDTH), lambda b, i: (b + boff, 0, 0)),
            pl.BlockSpec(memory_space=pl.ANY),
        ],
        out_specs=pl.BlockSpec((1, tq, MLA_WIDTH), lambda b, i: (b + boff, i, 0)),
        out_shape=jax.ShapeDtypeStruct(prev.shape, prev.dtype),
        input_output_aliases={3: 0},
        compiler_params=_cparams(("parallel", "arbitrary")),
        name="mla",
    )(q, k, v, prev)


def _na_base(i, rows):
    return jnp.clip(i * NA_RQ - NA_KH // 2, 0, rows - NA_WROWS)


def _na_kernel(q_ref, k_ref, v_ref, b_ref, prev_ref, o_ref):
    del prev_ref
    nq = q_ref.shape[1]
    rows = k_ref.shape[1] // GRID_W
    nk = NA_WROWS * GRID_W
    off = pl.multiple_of(_na_base(pl.program_id(1), rows) * GRID_W, GRID_W)
    lane = lax.broadcasted_iota(jnp.int32, (nq, LANE), 1)
    low = lane < NA_HEAD_DIM
    ones = jnp.ones((nk, LANE), BF16)
    for p in range(NA_HEADS // 2):
        sl = slice(p * LANE, (p + 1) * LANE)
        q2 = q_ref[0, :, sl]
        kw = k_ref[0, pl.ds(off, nk), sl]
        v_ones = jnp.concatenate([v_ref[0, pl.ds(off, nk), sl], ones], axis=1)
        outs = []
        for hh in range(2):
            qm = jnp.where(low, q2, 0) if hh == 0 else jnp.where(low, 0, q2)
            s = _dot_nt(qm, kw) + b_ref[0, 2 * p + hh]
            outs.append(_softmax_pv(s, v_ones))
        o_ref[0, :, sl] = jnp.where(low, outs[0], outs[1]).astype(o_ref.dtype)


def _na_call(q, k, v, bias, nb, t, boff, prev):
    rows = t // GRID_W
    nq = NA_RQ * GRID_W
    nk = NA_WROWS * GRID_W

    def bias_map(b, i):
        return ((i * NA_RQ - _na_base(i, rows)) // (NA_KH // 2), 0, 0, 0)

    return pl.pallas_call(
        _na_kernel,
        grid=(nb, rows // NA_RQ),
        in_specs=[
            pl.BlockSpec((1, nq, NA_WIDTH), lambda b, i: (b + boff, i, 0)),
            pl.BlockSpec((1, t, NA_WIDTH), lambda b, i: (b + boff, 0, 0)),
            pl.BlockSpec((1, t, NA_WIDTH), lambda b, i: (b + boff, 0, 0)),
            pl.BlockSpec((1, NA_HEADS, nq, nk), bias_map),
            pl.BlockSpec(memory_space=pl.ANY),
        ],
        out_specs=pl.BlockSpec((1, nq, NA_WIDTH), lambda b, i: (b + boff, i, 0)),
        out_shape=jax.ShapeDtypeStruct(prev.shape, prev.dtype),
        input_output_aliases={4: 0},
        compiler_params=_cparams(("parallel", "arbitrary")),
        name="natten",
    )(q, k, v, bias, prev)


def _na_bias_table(rpb):
    rows = 32
    c = np.arange(GRID_W)
    cstart = np.clip(c - NA_KW // 2, 0, GRID_W - NA_KW)
    col_ok = (c[None, :] >= cstart[:, None]) & (c[None, :] < cstart[:, None] + NA_KW)
    cidx = np.clip(c[None, :] - c[:, None] + NA_KW - 1, 0, 2 * NA_KW - 2)
    onehot = (cidx.reshape(-1)[None, :] == np.arange(2 * NA_KW - 1)[:, None]).astype(np.float32)
    tabs = []
    for i0 in (0, NA_RQ, rows - NA_RQ):
        base = int(np.clip(i0 - NA_KH // 2, 0, rows - NA_WROWS))
        qi = i0 + np.arange(NA_RQ)
        start = np.clip(qi - NA_KH // 2, 0, rows - NA_KH)
        kr = base + np.arange(NA_WROWS)
        row_ok = (kr[None, :] >= start[:, None]) & (kr[None, :] < start[:, None] + NA_KH)
        ridx = np.clip(kr[None, :] - qi[:, None] + NA_KH - 1, 0, 2 * NA_KH - 2)
        by_row = rpb[:, ridx, :].astype(F32)
        vals = jnp.einsum("hjrx,xq->hjrq", by_row, onehot, precision=lax.Precision.HIGHEST)
        vals = vals.reshape(NA_HEADS, NA_RQ, NA_WROWS, GRID_W, GRID_W).transpose(0, 1, 3, 2, 4)
        ok = row_ok[:, None, :, None] & col_ok[None, :, None, :]
        tab = jnp.where(ok[None], vals * LOG2E, NEG)
        tabs.append(tab.reshape(NA_HEADS, NA_RQ * GRID_W, NA_WROWS * GRID_W))
    return jnp.stack(tabs)


REC_ROWS = 16


def _route_tile(logits, tri, run):
    ne, tm = logits.shape
    row = lax.broadcasted_iota(jnp.int32, (ne, tm), 0)
    left = logits
    onehot = jnp.zeros((ne, tm), F32)
    vals, idxs = [], []
    for _ in range(TOP_K):
        m = jnp.max(left, axis=0, keepdims=True)
        idx = jnp.min(jnp.where(left == m, row, ne), axis=0, keepdims=True)
        sel = row == idx
        onehot = onehot + sel.astype(F32)
        left = jnp.where(sel, NEG, left)
        vals.append(m)
        idxs.append(idx)
    ex = [jnp.exp(v - vals[0]) for v in vals]
    den = (ex[0] + ex[1]) + (ex[2] + ex[3])
    before = _dot(onehot.astype(BF16), tri) + run
    ranks = [jnp.sum(jnp.where(row == idx, before, 0.0), axis=0, keepdims=True) for idx in idxs]
    rec = [idx.astype(F32) for idx in idxs] + ranks + [e / den for e in ex]
    rec.append(jnp.zeros((REC_ROWS - len(rec), tm), F32))
    return jnp.concatenate(rec, axis=0), jnp.sum(onehot, axis=1, keepdims=True)


def _mix_kernel(a_ref, n_ref, xa_ref, xb_ref, gm_ref, gn_ref, wo_ref, gf_ref, wrh_ref, wrl_ref, br_ref, tri_ref,
                x2_ref, hf_ref, rt_ref, cnt_ref, *, n0):
    @pl.when(pl.program_id(0) == 0)
    def _():
        cnt_ref[...] = jnp.zeros(cnt_ref.shape, F32)

    an = _rms(a_ref[...].astype(F32)) * gm_ref[...]
    nn = _rms(n_ref[...].astype(F32)) * gn_ref[...]
    cat = jnp.concatenate([an, nn], axis=-1).astype(BF16)
    x2 = _group_tile(xa_ref, xb_ref, n0) + _dot(cat, wo_ref[...])
    x2_ref[...] = x2
    hf = _rms(x2) * gf_ref[...]
    tm = hf.shape[0]
    for j in range(hf.shape[1] // LANE):
        hf_ref[pl.ds(j, tm, stride=SUB), :] = hf[:, j * LANE:(j + 1) * LANE]
    hi = hf.astype(BF16)
    lo = (hf - hi.astype(F32)).astype(BF16)
    logits = (_dot_nt(wrh_ref[...], hi) + (_dot_nt(wrh_ref[...], lo) + _dot_nt(wrl_ref[...], hi))) + br_ref[...]
    rt_ref[...], tile_cnt = _route_tile(logits, tri_ref[...], cnt_ref[:, 0:1])
    cnt_ref[...] += tile_cnt


def _mix_call(a, nat, xa, xb, g_mla, g_na, w_out, g_ffn, wr_hi, wr_lo, b_r):
    d = xa.shape[1]
    n = xa.shape[0] + xb.shape[0]
    tm = TOK_TILE
    n0 = xa.shape[0] // tm
    row = lambda i: (i, 0)
    fix = lambda i: (0, 0)
    tri = jnp.asarray(np.triu(np.ones((tm, tm), np.float32), 1), BF16)
    return pl.pallas_call(
        functools.partial(_mix_kernel, n0=n0),
        grid=(n // tm,),
        in_specs=[pl.BlockSpec((tm, MLA_WIDTH), row), pl.BlockSpec((tm, NA_WIDTH), row)] + _group_specs(tm, d, n0) + [
            pl.BlockSpec((1, MLA_WIDTH), fix),
            pl.BlockSpec((1, NA_WIDTH), fix),
            pl.BlockSpec(w_out.shape, fix),
            pl.BlockSpec((1, d), fix),
            pl.BlockSpec(wr_hi.shape, fix),
            pl.BlockSpec(wr_lo.shape, fix),
            pl.BlockSpec((N_EXPERTS, 1), fix),
            pl.BlockSpec((tm, tm), fix),
        ],
        out_specs=[pl.BlockSpec((tm, d), row), pl.BlockSpec((tm * SUB, LANE), row),
                   pl.BlockSpec((REC_ROWS, tm), lambda i: (0, i)), pl.BlockSpec((N_EXPERTS, LANE), fix)],
        out_shape=[jax.ShapeDtypeStruct((n, d), F32), jax.ShapeDtypeStruct((n * SUB, LANE), F32),
                   jax.ShapeDtypeStruct((REC_ROWS, n), F32), jax.ShapeDtypeStruct((N_EXPERTS, LANE), F32)],
        compiler_params=_cparams(("arbitrary",)),
        name="mix",
    )(a, nat, xa, xb, g_mla, g_na, w_out, g_ffn, wr_hi, wr_lo, b_r, tri)


def _gather_tiles(idx_ref, src_hbm, buf, sem, lo, hi):
    for r in range(lo, hi):
        src = src_hbm.at[pl.ds(pl.multiple_of(idx_ref[0, 0, r], SUB), SUB)]
        pltpu.make_async_copy(src, buf.at[pl.ds(r * SUB, SUB)], sem).start(priority=r % 2)


def _moe_kernel(be_ref, nu_ref, src_ref, srcn_ref, wgf_ref, bg_ref, wuf_ref, bu_ref, wdf_ref, bd_ref, hf_hbm,
                y_ref, xbuf, wg_ref, wu_ref, wd_ref, gsem):
    i = pl.program_id(0)
    nu = nu_ref[0]
    bm = xbuf.shape[1] // SUB
    nchunk = wd_ref.shape[1] // LANE
    blk = pl.ds(0, bm * SUB)

    def step(s):
        @pl.when(i >= nu)
        def _():
            y_ref[...] = jnp.zeros(y_ref.shape, F32)

        @pl.when(i == nu)
        def _():
            pltpu.make_async_copy(hf_hbm.at[blk], xbuf.at[s], gsem.at[s]).wait()

        @pl.when(i < nu)
        def _():
            pltpu.make_async_copy(hf_hbm.at[blk], xbuf.at[s], gsem.at[s]).wait()
            x = jnp.concatenate([xbuf[s, pl.ds(j, bm, stride=SUB), :] for j in range(nchunk)], axis=1).astype(BF16)
            per, cw = bm // MOE_PIECES, wg_ref.shape[1] // MOE_PIECES
            hs = []
            for c in range(MOE_PIECES):
                _gather_tiles(srcn_ref, hf_hbm, xbuf.at[1 - s], gsem.at[1 - s], c * per, (c + 1) * per)
                cs = slice(c * cw, (c + 1) * cw)
                g = _dot(x, wg_ref[:, cs]) + bg_ref[0][:, cs]
                u = _dot(x, wu_ref[:, cs]) + bu_ref[0][:, cs]
                g = jnp.minimum(g, SWIGLU_LIMIT)
                u = jnp.clip(u, -SWIGLU_LIMIT, SWIGLU_LIMIT)
                hs.append(((u + 1.0) * (g * jax.nn.sigmoid(SWIGLU_ALPHA * g))).astype(BF16))
            y = _dot(jnp.concatenate(hs, axis=1), wd_ref[...]) + bd_ref[0]
            for j in range(nchunk):
                y_ref[pl.ds(j, bm, stride=SUB), :] = y[:, j * LANE:(j + 1) * LANE]

    @pl.when(i == 0)
    def _():
        _gather_tiles(src_ref, hf_hbm, xbuf.at[0], gsem.at[0], 0, bm)

    blk_i = jnp.minimum(i, nu - 1)

    @pl.when((i == 0) | ((i < nu) & (be_ref[blk_i] != be_ref[jnp.maximum(blk_i - 1, 0)])))
    def _():
        wg_ref[...] = wgf_ref[0].astype(BF16)
        wu_ref[...] = wuf_ref[0].astype(BF16)
        wd_ref[...] = wdf_ref[0].astype(BF16)

    for s in range(2):
        pl.when(lax.rem(i, 2) == s)(functools.partial(step, s))


def _moe_call(blk_e, nused, src, wg, bg, wu, bu, wd, bd, hf):
    nblk, _, bm = src.shape
    d = wg.shape[1]
    dff = wg.shape[2]
    assert d == SUB * LANE and hf.shape[1] == LANE
    last = lambda i, nu: jnp.minimum(i, nu[0] - 1)
    cur3 = lambda i, be, nu: (last(i, nu), 0, 0)
    nxt3 = lambda i, be, nu: (last(i + 1, nu), 0, 0)
    wmap = lambda i, be, nu: (be[last(i, nu)], 0, 0)
    smem = functools.partial(pl.BlockSpec, (1, 1, bm), memory_space=pltpu.SMEM)
    grid_spec = pltpu.PrefetchScalarGridSpec(
        num_scalar_prefetch=2,
        grid=(nblk,),
        in_specs=[
            smem(cur3), smem(nxt3),
            pl.BlockSpec((1, d, dff), wmap),
            pl.BlockSpec((1, 1, dff), wmap),
            pl.BlockSpec((1, d, dff), wmap),
            pl.BlockSpec((1, 1, dff), wmap),
            pl.BlockSpec((1, dff, d), wmap),
            pl.BlockSpec((1, 1, d), wmap),
            pl.BlockSpec(memory_space=pl.ANY),
        ],
        out_specs=pl.BlockSpec((bm * SUB, LANE), lambda i, be, nu: (i, 0)),
        scratch_shapes=[
            pltpu.VMEM((2, bm * SUB, LANE), F32),
            pltpu.VMEM((d, dff), BF16),
            pltpu.VMEM((d, dff), BF16),
            pltpu.VMEM((dff, d), BF16),
            pltpu.SemaphoreType.DMA((2,)),
        ],
    )
    return pl.pallas_call(
        _moe_kernel,
        grid_spec=grid_spec,
        out_shape=jax.ShapeDtypeStruct((nblk * bm * SUB, LANE), F32),
        compiler_params=_cparams(("arbitrary",)),
        name="moe",
    )(blk_e, nused, src, src, wg, bg, wu, bu, wd, bd, hf)


def _final_kernel(pos_ref, posn_ref, x_ref, gt_ref, g_ref, ys_hbm, o_ref, ybuf, gsem, *, nt):
    i = pl.program_id(0)
    tm = x_ref.shape[0]
    cnt = TOP_K * tm

    def step(s):
        @pl.when(i + 1 < nt)
        def _():
            _gather_tiles(posn_ref, ys_hbm, ybuf.at[1 - s], gsem.at[1 - s], 0, cnt)

        pltpu.make_async_copy(ys_hbm.at[pl.ds(0, cnt * SUB)], ybuf.at[s], gsem.at[s]).wait()
        gates = gt_ref[...]
        gk = [jnp.broadcast_to(gates[:, k:k + 1], (tm, LANE)) for k in range(TOP_K)]
        chunks = []
        for j in range(x_ref.shape[1] // LANE):
            parts = [gk[k] * ybuf[s, pl.ds(k * tm * SUB + j, tm, stride=SUB), :] for k in range(TOP_K)]
            chunks.append((parts[0] + parts[1]) + (parts[2] + parts[3]))
        o_ref[...] = _rms(x_ref[...] + jnp.concatenate(chunks, axis=1)) * g_ref[...]

    @pl.when(i == 0)
    def _():
        _gather_tiles(pos_ref, ys_hbm, ybuf.at[0], gsem.at[0], 0, cnt)

    for s in range(2):
        pl.when(lax.rem(i, 2) == s)(functools.partial(step, s))


def _final_call(x2, ys, pos_tiles, gates, g_final, tok0, ntok):
    d = x2.shape[1]
    assert TOP_K == 4 and d == SUB * LANE
    tm = FIN_TM
    assert tok0 % tm == 0 and ntok % tm == 0
    off, nt = tok0 // tm, ntok // tm
    smem = functools.partial(pl.BlockSpec, (1, 1, TOP_K * tm), memory_space=pltpu.SMEM)
    return pl.pallas_call(
        functools.partial(_final_kernel, nt=nt),
        grid=(nt,),
        in_specs=[
            smem(lambda i: (off + i, 0, 0)),
            smem(lambda i: (off + jnp.minimum(i + 1, nt - 1), 0, 0)),
            pl.BlockSpec((tm, d), lambda i: (off + i, 0)),
            pl.BlockSpec((tm, TOP_K), lambda i: (off + i, 0)),
            pl.BlockSpec((1, d), lambda i: (0, 0)),
            pl.BlockSpec(memory_space=pl.ANY),
        ],
        out_specs=pl.BlockSpec((tm, d), lambda i: (i, 0)),
        out_shape=jax.ShapeDtypeStruct((ntok, d), F32),
        scratch_shapes=[pltpu.VMEM((2, TOP_K * tm * SUB, LANE), F32), pltpu.SemaphoreType.DMA((2,))],
        compiler_params=_cparams(("arbitrary",)),
        name="final",
    )(pos_tiles, pos_tiles, x2, gates, g_final, ys)


def _prep_weights(w_in, w_uq, w_ukv, t_max):
    d = w_in.shape[0]
    o = np.cumsum((0, Q_LORA, KV_LORA, ROPE_DIM, NA_WIDTH, NA_WIDTH, NA_WIDTH))
    w_cq, w_ckv, w_kpe = w_in[:, o[0]:o[1]], w_in[:, o[1]:o[2]], w_in[:, o[2]:o[3]]
    w_naq, w_nak, w_nav = w_in[:, o[3]:o[4]], w_in[:, o[4]:o[5]], w_in[:, o[5]:o[6]]
    half = ROPE_DIM // 2
    swap = np.concatenate([np.arange(half, ROPE_DIM), np.arange(half)])
    zpad = lambda w, lo: jnp.pad(w, ((0, 0), (lo, LANE - lo - w.shape[1])))
    win_r = jnp.concatenate([w_naq, w_nak, w_nav, w_cq, w_ckv,
                             zpad(w_kpe, QK_NOPE), zpad(w_kpe[:, swap], QK_NOPE)], axis=1).astype(BF16)

    uq = w_uq.reshape(Q_LORA, MLA_HEADS, QK_HEAD)
    wqa = jnp.pad(uq, ((0, 0), (0, 0), (0, LANE - QK_HEAD))).reshape(Q_LORA, MLA_HEADS * LANE).astype(BF16)
    wqb = jnp.pad(uq[:, :, QK_NOPE:][:, :, swap], ((0, 0), (0, 0), (QK_NOPE, LANE - QK_HEAD)))
    wqb = wqb.reshape(Q_LORA, MLA_HEADS * LANE).astype(BF16)
    ukv = w_ukv.reshape(KV_LORA, MLA_HEADS, QK_NOPE + V_HEAD)
    wuk = jnp.pad(ukv[:, :, :QK_NOPE], ((0, 0), (0, 0), (0, LANE - QK_NOPE)))
    wuk = wuk.reshape(KV_LORA, MLA_HEADS * LANE).astype(BF16)
    wuv = ukv[:, :, QK_NOPE:].reshape(KV_LORA, MLA_WIDTH).astype(BF16)

    pos = jnp.arange(t_max, dtype=F32)
    inv = 1.0 / (ROPE_THETA ** (jnp.arange(0, ROPE_DIM, 2, dtype=F32) / ROPE_DIM))
    ang = pos[:, None] * inv[None, :]
    cos, sin = jnp.cos(ang), jnp.sin(ang)
    c32 = jnp.concatenate([cos, cos], axis=1)
    s32 = jnp.concatenate([-sin, sin], axis=1)
    z = lambda w: jnp.zeros((t_max, w), F32)
    cak = jnp.concatenate([z(QK_NOPE), c32, z(LANE - QK_HEAD)], axis=1)
    sbk = jnp.concatenate([z(QK_NOPE), s32, z(LANE - QK_HEAD)], axis=1)
    scale = QK_HEAD ** -0.5 * LOG2E
    caq = scale * jnp.concatenate([jnp.ones((t_max, QK_NOPE), F32), c32, z(LANE - QK_HEAD)], axis=1)
    tab = jnp.concatenate([caq, scale * sbk, cak, sbk], axis=1)
    del d
    return win_r, wqa, wqb, wuk, wuv, tab


def _route(rec, cnt, bm, tm):
    n = rec.shape[1]
    a = n * TOP_K
    nblk = a // bm + N_EXPERTS
    expert = rec[0:TOP_K].astype(jnp.int32)
    rank = rec[TOP_K:2 * TOP_K].astype(jnp.int32)
    gates = rec[2 * TOP_K:3 * TOP_K].T
    counts = cnt[:, 0].astype(jnp.int32)
    start = jnp.cumsum(counts) - counts
    padded = ((counts + bm - 1) // bm) * bm
    pad_end = jnp.cumsum(padded)
    pad_start = pad_end - padded
    blk0 = jnp.arange(nblk, dtype=jnp.int32) * bm
    blk_e = jnp.minimum(jnp.sum((pad_end[None, :] <= blk0[:, None]).astype(jnp.int32), axis=1), N_EXPERTS - 1)
    nused = (pad_end[-1:] // bm).astype(jnp.int32)
    experts = jnp.arange(N_EXPERTS, dtype=jnp.int32)[:, None, None]
    seg = jnp.sum(jnp.where(expert[None] == experts, pad_start[:, None, None], 0), axis=0)
    pos = seg + rank
    pos_tiles = (pos * SUB).reshape(TOP_K, n // tm, tm).transpose(1, 0, 2).reshape(n // tm, 1, TOP_K * tm)
    tok = jnp.broadcast_to(jnp.arange(n, dtype=jnp.int32)[None, :], (TOP_K, n))
    _, tok_sorted = lax.sort((pos.reshape(-1), tok.reshape(-1)), num_keys=1)
    slot_rank = (blk0 - pad_start[blk_e])[:, None] + jnp.arange(bm, dtype=jnp.int32)[None, :]
    valid = slot_rank < counts[blk_e][:, None]
    src = jnp.where(valid, tok_sorted[jnp.clip(start[blk_e][:, None] + slot_rank, 0, a - 1)], 0)
    return (src * SUB)[:, None, :], blk_e.astype(jnp.int32), nused, pos_tiles, gates


def kernel(x_prompt, x_sample, g_attn_norm, w_in, g_q_lora, w_uq, g_kv_lora, w_ukv, na_rpb, g_out_mla,
           g_out_na, w_out, g_ffn_norm, w_router, b_router, w_gate, b_gate, w_up, b_up, w_down, b_down,
           g_final):
    assert g_attn_norm.shape[0] == 1, "single-layer trunk"
    d = x_prompt.shape[-1]
    groups = [x_prompt, x_sample]
    shapes = [(g.shape[0], g.shape[1]) for g in groups]
    for _, t in shapes:
        assert t % TOK_TILE == 0 and t % GRID_W == 0 and (t // GRID_W) % NA_RQ == 0
        assert t // GRID_W >= NA_WROWS + NA_RQ
    xa, xb = (g.reshape(-1, d) for g in groups)
    n = xa.shape[0] + xb.shape[0]
    t_max = max(t for _, t in shapes)

    win_r, wqa, wqb, wuk, wuv, tab = _prep_weights(w_in[0], w_uq[0], w_ukv[0], t_max)
    pos_blocks = [(b * t // TOK_TILE, t // TOK_TILE) for b, t in shapes]
    row = lambda v: v.reshape(1, -1)
    q, k, v, naq, nak, nav = _proj_call(xa, xb, row(g_attn_norm[0]), win_r, row(g_q_lora[0]), wqa, wqb,
                                        row(g_kv_lora[0]), wuk, wuv, tab, pos_blocks)

    bias = _na_bias_table(na_rpb[0])
    a_all = jnp.zeros((n, MLA_WIDTH), BF16)
    n_all = jnp.zeros((n, NA_WIDTH), BF16)
    tok0 = 0
    for b, t in shapes:
        assert tok0 % t == 0 and n % t == 0
        boff = tok0 // t
        view = lambda arr: arr.reshape(n // t, t, arr.shape[-1])
        a_all = _mla_call(view(q), view(k), view(v), b, t, boff, view(a_all)).reshape(n, MLA_WIDTH)
        n_all = _na_call(view(naq), view(nak), view(nav), bias, b, t, boff, view(n_all)).reshape(n, NA_WIDTH)
        tok0 += b * t

    wr = w_router[0].T
    wr_hi = wr.astype(BF16)
    wr_lo = (wr - wr_hi.astype(F32)).astype(BF16)
    b_r = b_router[0].reshape(N_EXPERTS, 1)
    x2, hf, rec, cnt = _mix_call(a_all, n_all, xa, xb, row(g_out_mla[0]), row(g_out_na[0]), w_out[0].astype(BF16),
                                 row(g_ffn_norm[0]), wr_hi, wr_lo, b_r)

    src, blk_e, nused, pos_tiles, gates = _route(rec, cnt, MOE_BM, FIN_TM)
    ys = _moe_call(blk_e, nused, src, w_gate[0], b_gate[0][:, None, :], w_up[0], b_up[0][:, None, :],
                   w_down[0], b_down[0][:, None, :], hf)

    outs, tok0 = [], 0
    for b, t in shapes:
        outs.append(_final_call(x2, ys, pos_tiles, gates, row(g_final), tok0, b * t).reshape(b, t, d))
        tok0 += b * t
    return tuple(outs)
```

```python
import functools

import numpy as np
import jax
import jax.numpy as jnp
from jax import lax
from jax.experimental import pallas as pl
from jax.experimental.pallas import tpu as pltpu

F32 = jnp.float32
BF16 = jnp.bfloat16

EPS = 1e-6
MLA_HEADS = 8
Q_LORA = 256
KV_LORA = 128
QK_NOPE = 64
ROPE_DIM = 32
QK_HEAD = QK_NOPE + ROPE_DIM
V_HEAD = 64
ROPE_THETA = 10000.0
NA_HEADS = 8
NA_HEAD_DIM = 64
NA_WIDTH = NA_HEADS * NA_HEAD_DIM
MLA_WIDTH = MLA_HEADS * V_HEAD
GRID_W = 64
NA_KH = 8
NA_KW = 16
N_EXPERTS = 32
TOP_K = 4
SWIGLU_LIMIT = 7.0
SWIGLU_ALPHA = 1.702

LANE = 128
SUB = 8
VMEM_LIMIT = 56 * 1024 * 1024

TOK_TILE = 512
MLA_TQ = 256
LOG2E = 1.4426950408889634
NA_RQ = 4
NA_WROWS = 12
MOE_BM = 512
MOE_PIECES = 4
FIN_TM = 256
NEG = -1e30

IN_COLS = 3 * NA_WIDTH + Q_LORA + KV_LORA + LANE


def _rms(x):
    return x * lax.rsqrt(jnp.mean(x * x, axis=-1, keepdims=True) + EPS)


def _dot(a, b):
    return jnp.dot(a, b, preferred_element_type=F32)


def _dot_nt(a, b):
    return lax.dot_general(a, b, (((1,), (1,)), ((), ())), preferred_element_type=F32)


def _cparams(sem):
    return pltpu.CompilerParams(dimension_semantics=sem, vmem_limit_bytes=VMEM_LIMIT)


def _group_tile(xa_ref, xb_ref, n0):
    return jnp.where(pl.program_id(0) < n0, xa_ref[...], xb_ref[...])


def _group_specs(tm, d, n0):
    return [pl.BlockSpec((tm, d), lambda i: (jnp.minimum(i, n0 - 1), 0)),
            pl.BlockSpec((tm, d), lambda i: (jnp.maximum(i - n0, 0), 0))]


def _proj_kernel(xa_ref, xb_ref, g_ref, win_ref, gq_ref, wq_ref, gkv_ref, wuk_ref, wuv_ref, tab_ref,
                 q_ref, k_ref, v_ref, naq_ref, nak_ref, nav_ref, *, n0):
    h = (_rms(_group_tile(xa_ref, xb_ref, n0)) * g_ref[...]).astype(BF16)
    proj = _dot(h, win_ref[...])
    o = 0
    naq_ref[...] = (proj[:, o:o + NA_WIDTH] * (NA_HEAD_DIM ** -0.5 * LOG2E)).astype(BF16)
    o += NA_WIDTH
    nak_ref[...] = proj[:, o:o + NA_WIDTH].astype(BF16)
    o += NA_WIDTH
    nav_ref[...] = proj[:, o:o + NA_WIDTH].astype(BF16)
    o += NA_WIDTH
    cq = (_rms(proj[:, o:o + Q_LORA]) * gq_ref[...]).astype(BF16)
    o += Q_LORA
    ckv = (_rms(proj[:, o:o + KV_LORA]) * gkv_ref[...]).astype(BF16)
    o += KV_LORA
    kpe_blk = proj[:, o:o + LANE]

    tab = tab_ref[...]
    caq, sbq = tab[:, 0:LANE], tab[:, LANE:2 * LANE]
    cak, sbk = tab[:, 2 * LANE:3 * LANE], tab[:, 3 * LANE:4 * LANE]

    swapped = lambda slot: pltpu.roll(slot, LANE - ROPE_DIM, 1)
    q_all = _dot(cq, wq_ref[...])
    kn = _dot(ckv, wuk_ref[...])
    kpe = kpe_blk * cak + swapped(kpe_blk) * sbk
    for hd in range(MLA_HEADS):
        sl = slice(hd * LANE, (hd + 1) * LANE)
        qs = q_all[:, sl]
        q_ref[:, sl] = (qs * caq + swapped(qs) * sbq).astype(BF16)
        k_ref[:, sl] = (kn[:, sl] + kpe).astype(BF16)
    v_ref[...] = _dot(ckv, wuv_ref[...]).astype(BF16)


def _proj_call(xa, xb, g_attn, win_r, g_q, wq, g_kv, wuk, wuv, tab, pos_blocks):
    d = xa.shape[1]
    n = xa.shape[0] + xb.shape[0]
    tm = TOK_TILE
    (n0, pb0), (_, pb1) = pos_blocks

    def tab_map(i):
        return (jnp.where(i < n0, i % pb0, (i - n0) % pb1), 0)

    row = lambda i: (i, 0)
    fix = lambda i: (0, 0)
    out_w = (MLA_HEADS * LANE, MLA_HEADS * LANE, MLA_WIDTH, NA_WIDTH, NA_WIDTH, NA_WIDTH)
    return pl.pallas_call(
        functools.partial(_proj_kernel, n0=n0),
        grid=(n // tm,),
        in_specs=_group_specs(tm, d, n0) + [
            pl.BlockSpec((1, d), fix),
            pl.BlockSpec(win_r.shape, fix),
            pl.BlockSpec((1, Q_LORA), fix),
            pl.BlockSpec(wq.shape, fix),
            pl.BlockSpec((1, KV_LORA), fix),
            pl.BlockSpec(wuk.shape, fix),
            pl.BlockSpec(wuv.shape, fix),
            pl.BlockSpec((tm, 4 * LANE), tab_map),
        ],
        out_specs=[pl.BlockSpec((tm, w), row) for w in out_w],
        out_shape=[jax.ShapeDtypeStruct((n, w), BF16) for w in out_w],
        compiler_params=_cparams(("parallel",)),
        name="proj",
    )(xa, xb, g_attn, win_r, g_q, wq, g_kv, wuk, wuv, tab)


def _softmax_pv(s, v_ones):
    p = jnp.exp2(s - jnp.max(s, axis=-1, keepdims=True))
    acc = _dot(p.astype(BF16), v_ones)
    return acc[:, :LANE] / acc[:, LANE:]


def _mla_kernel(q_ref, k_ref, v_ref, prev_ref, o_ref):
    del prev_ref
    tq, t = q_ref.shape[1], k_ref.shape[1]
    low = lax.broadcasted_iota(jnp.int32, (tq, LANE), 1) < V_HEAD
    ones = jnp.ones((t, LANE), BF16)
    for pp in range(MLA_HEADS // 2):
        v_ones = jnp.concatenate([v_ref[0, :, pp * LANE:(pp + 1) * LANE], ones], axis=1)
        res = []
        for h in (2 * pp, 2 * pp + 1):
            s = _dot_nt(q_ref[0, :, h * LANE:(h + 1) * LANE], k_ref[0, :, h * LANE:(h + 1) * LANE])
            res.append(_softmax_pv(s, v_ones))
        o_ref[0, :, pp * LANE:(pp + 1) * LANE] = jnp.where(low, res[0], res[1]).astype(o_ref.dtype)


def _mla_call(q, k, v, nb, t, boff, prev):
    tq = min(MLA_TQ, t)
    return pl.pallas_call(
        _mla_kernel,
        grid=(nb, t // tq),
        in_specs=[
            pl.BlockSpec((1, tq, MLA_HEADS * LANE), lambda b, i: (b + boff, i, 0)),
            pl.BlockSpec((1, t, MLA_HEADS * LANE), lambda b, i: (b + boff, 0, 0)),
            pl.BlockSpec((1, t, MLA_WIDTH), lambda b, i: (b + boff, 0, 0)),
            pl.BlockSpec(memory_space=pl.ANY),
        ],
        out_specs=pl.BlockSpec((1, tq, MLA_WIDTH), lambda b, i: (b + boff, i, 0)),
        out_shape=jax.ShapeDtypeStruct(prev.shape, prev.dtype),
        input_output_aliases={3: 0},
        compiler_params=_cparams(("parallel", "arbitrary")),
        name="mla",
    )(q, k, v, prev)


def _na_base(i, rows):
    return jnp.clip(i * NA_RQ - NA_KH // 2, 0, rows - NA_WROWS)


def _na_kernel(q_ref, k_ref, v_ref, b_ref, prev_ref, o_ref):
    del prev_ref
    nq = q_ref.shape[1]
    rows = k_ref.shape[1] // GRID_W
    nk = NA_WROWS * GRID_W
    off = pl.multiple_of(_na_base(pl.program_id(1), rows) * GRID_W, GRID_W)
    lane = lax.broadcasted_iota(jnp.int32, (nq, LANE), 1)
    low = lane < NA_HEAD_DIM
    ones = jnp.ones((nk, LANE), BF16)
    for p in range(NA_HEADS // 2):
        sl = slice(p * LANE, (p + 1) * LANE)
        q2 = q_ref[0, :, sl]
        kw = k_ref[0, pl.ds(off, nk), sl]
        v_ones = jnp.concatenate([v_ref[0, pl.ds(off, nk), sl], ones], axis=1)
        outs = []
        for hh in range(2):
            qm = jnp.where(low, q2, 0) if hh == 0 else jnp.where(low, 0, q2)
            s = _dot_nt(qm, kw) + b_ref[0, 2 * p + hh]
            outs.append(_softmax_pv(s, v_ones))
        o_ref[0, :, sl] = jnp.where(low, outs[0], outs[1]).astype(o_ref.dtype)


def _na_call(q, k, v, bias, nb, t, boff, prev):
    rows = t // GRID_W
    nq = NA_RQ * GRID_W
    nk = NA_WROWS * GRID_W

    def bias_map(b, i):
        return ((i * NA_RQ - _na_base(i, rows)) // (NA_KH // 2), 0, 0, 0)

    return pl.pallas_call(
        _na_kernel,
        grid=(nb, rows // NA_RQ),
        in_specs=[
            pl.BlockSpec((1, nq, NA_WIDTH), lambda b, i: (b + boff, i, 0)),
            pl.BlockSpec((1, t, NA_WIDTH), lambda b, i: (b + boff, 0, 0)),
            pl.BlockSpec((1, t, NA_WIDTH), lambda b, i: (b + boff, 0, 0)),
            pl.BlockSpec((1, NA_HEADS, nq, nk), bias_map),
            pl.BlockSpec(memory_space=pl.ANY),
        ],
        out_specs=pl.BlockSpec((1, nq, NA_WIDTH), lambda b, i: (b + boff, i, 0)),
        out_shape=jax.ShapeDtypeStruct(prev.shape, prev.dtype),
        input_output_aliases={4: 0},
        compiler_params=_cparams(("parallel", "arbitrary")),
        name="natten",
    )(q, k, v, bias, prev)


def _na_bias_table(rpb):
    rows = 32
    c = np.arange(GRID_W)
    cstart = np.clip(c - NA_KW // 2, 0, GRID_W - NA_KW)
    col_ok = (c[None, :] >= cstart[:, None]) & (c[None, :] < cstart[:, None] + NA_KW)
    cidx = np.clip(c[None, :] - c[:, None] + NA_KW - 1, 0, 2 * NA_KW - 2)
    onehot = (cidx.reshape(-1)[None, :] == np.arange(2 * NA_KW - 1)[:, None]).astype(np.float32)
    tabs = []
    for i0 in (0, NA_RQ, rows - NA_RQ):
        base = int(np.clip(i0 - NA_KH // 2, 0, rows - NA_WROWS))
        qi = i0 + np.arange(NA_RQ)
        start = np.clip(qi - NA_KH // 2, 0, rows - NA_KH)
        kr = base + np.arange(NA_WROWS)
        row_ok = (kr[None, :] >= start[:, None]) & (kr[None, :] < start[:, None] + NA_KH)
        ridx = np.clip(kr[None, :] - qi[:, None] + NA_KH - 1, 0, 2 * NA_KH - 2)
        by_row = rpb[:, ridx, :].astype(F32)
        vals = jnp.einsum("hjrx,xq->hjrq", by_row, onehot, precision=lax.Precision.HIGHEST)
        vals = vals.reshape(NA_HEADS, NA_RQ, NA_WROWS, GRID_W, GRID_W).transpose(0, 1, 3, 2, 4)
        ok = row_ok[:, None, :, None] & col_ok[None, :, None, :]
        tab = jnp.where(ok[None], vals * LOG2E, NEG)
        tabs.append(tab.reshape(NA_HEADS, NA_RQ * GRID_W, NA_WROWS * GRID_W))
    return jnp.stack(tabs)


REC_ROWS = 16


def _route_tile(logits, tri, run):
    ne, tm = logits.shape
    row = lax.broadcasted_iota(jnp.int32, (ne, tm), 0)
    left = logits
    onehot = jnp.zeros((ne, tm), F32)
    vals, idxs = [], []
    for _ in range(TOP_K):
        m = jnp.max(left, axis=0, keepdims=True)
        idx = jnp.min(jnp.where(left == m, row, ne), axis=0, keepdims=True)
        sel = row == idx
        onehot = onehot + sel.astype(F32)
        left = jnp.where(sel, NEG, left)
        vals.append(m)
        idxs.append(idx)
    ex = [jnp.exp(v - vals[0]) for v in vals]
    den = (ex[0] + ex[1]) + (ex[2] + ex[3])
    before = _dot(onehot.astype(BF16), tri) + run
    ranks = [jnp.sum(jnp.where(row == idx, before, 0.0), axis=0, keepdims=True) for idx in idxs]
    rec = [idx.astype(F32) for idx in idxs] + ranks + [e / den for e in ex]
    rec.append(jnp.zeros((REC_ROWS - len(rec), tm), F32))
    return jnp.concatenate(rec, axis=0), jnp.sum(onehot, axis=1, keepdims=True)


def _mix_kernel(a_ref, n_ref, xa_ref, xb_ref, gm_ref, gn_ref, wo_ref, gf_ref, wrh_ref, wrl_ref, br_ref, tri_ref,
                x2_ref, hf_ref, rt_ref, cnt_ref, *, n0):
    @pl.when(pl.program_id(0) == 0)
    def _():
        cnt_ref[...] = jnp.zeros(cnt_ref.shape, F32)

    an = _rms(a_ref[...].astype(F32)) * gm_ref[...]
    nn = _rms(n_ref[...].astype(F32)) * gn_ref[...]
    cat = jnp.concatenate([an, nn], axis=-1).astype(BF16)
    x2 = _group_tile(xa_ref, xb_ref, n0) + _dot(cat, wo_ref[...])
    x2_ref[...] = x2
    hf = _rms(x2) * gf_ref[...]
    tm = hf.shape[0]
    for j in range(hf.shape[1] // LANE):
        hf_ref[pl.ds(j, tm, stride=SUB), :] = hf[:, j * LANE:(j + 1) * LANE]
    hi = hf.astype(BF16)
    lo = (hf - hi.astype(F32)).astype(BF16)
    logits = (_dot_nt(wrh_ref[...], hi) + (_dot_nt(wrh_ref[...], lo) + _dot_nt(wrl_ref[...], hi))) + br_ref[...]
    rt_ref[...], tile_cnt = _route_tile(logits, tri_ref[...], cnt_ref[:, 0:1])
    cnt_ref[...] += tile_cnt


def _mix_call(a, nat, xa, xb, g_mla, g_na, w_out, g_ffn, wr_hi, wr_lo, b_r):
    d = xa.shape[1]
    n = xa.shape[0] + xb.shape[0]
    tm = TOK_TILE
    n0 = xa.shape[0] // tm
    row = lambda i: (i, 0)
    fix = lambda i: (0, 0)
    tri = jnp.asarray(np.triu(np.ones((tm, tm), np.float32), 1), BF16)
    return pl.pallas_call(
        functools.partial(_mix_kernel, n0=n0),
        grid=(n // tm,),
        in_specs=[pl.BlockSpec((tm, MLA_WIDTH), row), pl.BlockSpec((tm, NA_WIDTH), row)] + _group_specs(tm, d, n0) + [
            pl.BlockSpec((1, MLA_WIDTH), fix),
            pl.BlockSpec((1, NA_WIDTH), fix),
            pl.BlockSpec(w_out.shape, fix),
            pl.BlockSpec((1, d), fix),
            pl.BlockSpec(wr_hi.shape, fix),
            pl.BlockSpec(wr_lo.shape, fix),
            pl.BlockSpec((N_EXPERTS, 1), fix),
            pl.BlockSpec((tm, tm), fix),
        ],
        out_specs=[pl.BlockSpec((tm, d), row), pl.BlockSpec((tm * SUB, LANE), row),
                   pl.BlockSpec((REC_ROWS, tm), lambda i: (0, i)), pl.BlockSpec((N_EXPERTS, LANE), fix)],
        out_shape=[jax.ShapeDtypeStruct((n, d), F32), jax.ShapeDtypeStruct((n * SUB, LANE), F32),
                   jax.ShapeDtypeStruct((REC_ROWS, n), F32), jax.ShapeDtypeStruct((N_EXPERTS, LANE), F32)],
        compiler_params=_cparams(("arbitrary",)),
        name="mix",
    )(a, nat, xa, xb, g_mla, g_na, w_out, g_ffn, wr_hi, wr_lo, b_r, tri)


def _gather_tiles(idx_ref, src_hbm, buf, sem, lo, hi):
    for r in range(lo, hi):
        src = src_hbm.at[pl.ds(pl.multiple_of(idx_ref[0, 0, r], SUB), SUB)]
        pltpu.make_async_copy(src, buf.at[pl.ds(r * SUB, SUB)], sem).start(priority=r % 2)


def _moe_kernel(be_ref, nu_ref, src_ref, srcn_ref, wgf_ref, bg_ref, wuf_ref, bu_ref, wdf_ref, bd_ref, hf_hbm,
                y_ref, xbuf, wg_ref, wu_ref, wd_ref, gsem):
    i = pl.program_id(0)
    nu = nu_ref[0]
    bm = xbuf.shape[1] // SUB
    nchunk = wd_ref.shape[1] // LANE
    blk = pl.ds(0, bm * SUB)

    def step(s):
        @pl.when(i >= nu)
        def _():
            y_ref[...] = jnp.zeros(y_ref.shape, F32)

        @pl.when(i == nu)
        def _():
            pltpu.make_async_copy(hf_hbm.at[blk], xbuf.at[s], gsem.at[s]).wait()

        @pl.when(i < nu)
        def _():
            pltpu.make_async_copy(hf_hbm.at[blk], xbuf.at[s], gsem.at[s]).wait()
            x = jnp.concatenate([xbuf[s, pl.ds(j, bm, stride=SUB), :] for j in range(nchunk)], axis=1).astype(BF16)
            per, cw = bm // MOE_PIECES, wg_ref.shape[1] // MOE_PIECES
            hs = []
            for c in range(MOE_PIECES):
                _gather_tiles(srcn_ref, hf_hbm, xbuf.at[1 - s], gsem.at[1 - s], c * per, (c + 1) * per)
                cs = slice(c * cw, (c + 1) * cw)
                g = _dot(x, wg_ref[:, cs]) + bg_ref[0][:, cs]
                u = _dot(x, wu_ref[:, cs]) + bu_ref[0][:, cs]
                g = jnp.minimum(g, SWIGLU_LIMIT)
                u = jnp.clip(u, -SWIGLU_LIMIT, SWIGLU_LIMIT)
                hs.append(((u + 1.0) * (g * jax.nn.sigmoid(SWIGLU_ALPHA * g))).astype(BF16))
            y = _dot(jnp.concatenate(hs, axis=1), wd_ref[...]) + bd_ref[0]
            for j in range(nchunk):
                y_ref[pl.ds(j, bm, stride=SUB), :] = y[:, j * LANE:(j + 1) * LANE]

    @pl.when(i == 0)
    def _():
        _gather_tiles(src_ref, hf_hbm, xbuf.at[0], gsem.at[0], 0, bm)

    blk_i = jnp.minimum(i, nu - 1)

    @pl.when((i == 0) | ((i < nu) & (be_ref[blk_i] != be_ref[jnp.maximum(blk_i - 1, 0)])))
    def _():
        wg_ref[...] = wgf_ref[0].astype(BF16)
        wu_ref[...] = wuf_ref[0].astype(BF16)
        wd_ref[...] = wdf_ref[0].astype(BF16)

    for s in range(2):
        pl.when(lax.rem(i, 2) == s)(functools.partial(step, s))


def _moe_call(blk_e, nused, src, wg, bg, wu, bu, wd, bd, hf):
    nblk, _, bm = src.shape
    d = wg.shape[1]
    dff = wg.shape[2]
    assert d == SUB * LANE and hf.shape[1] == LANE
    last = lambda i, nu: jnp.minimum(i, nu[0] - 1)
    cur3 = lambda i, be, nu: (last(i, nu), 0, 0)
    nxt3 = lambda i, be, nu: (last(i + 1, nu), 0, 0)
    wmap = lambda i, be, nu: (be[last(i, nu)], 0, 0)
    smem = functools.partial(pl.BlockSpec, (1, 1, bm), memory_space=pltpu.SMEM)
    grid_spec = pltpu.PrefetchScalarGridSpec(
        num_scalar_prefetch=2,
        grid=(nblk,),
        in_specs=[
            smem(cur3), smem(nxt3),
            pl.BlockSpec((1, d, dff), wmap),
            pl.BlockSpec((1, 1, dff), wmap),
            pl.BlockSpec((1, d, dff), wmap),
            pl.BlockSpec((1, 1, dff), wmap),
            pl.BlockSpec((1, dff, d), wmap),
            pl.BlockSpec((1, 1, d), wmap),
            pl.BlockSpec(memory_space=pl.ANY),
        ],
        out_specs=pl.BlockSpec((bm * SUB, LANE), lambda i, be, nu: (i, 0)),
        scratch_shapes=[
            pltpu.VMEM((2, bm * SUB, LANE), F32),
            pltpu.VMEM((d, dff), BF16),
            pltpu.VMEM((d, dff), BF16),
            pltpu.VMEM((dff, d), BF16),
            pltpu.SemaphoreType.DMA((2,)),
        ],
    )
    return pl.pallas_call(
        _moe_kernel,
        grid_spec=grid_spec,
        out_shape=jax.ShapeDtypeStruct((nblk * bm * SUB, LANE), F32),
        compiler_params=_cparams(("arbitrary",)),
        name="moe",
    )(blk_e, nused, src, src, wg, bg, wu, bu, wd, bd, hf)


def _final_kernel(pos_ref, posn_ref, x_ref, gt_ref, g_ref, ys_hbm, o_ref, ybuf, gsem, *, nt):
    i = pl.program_id(0)
    tm = x_ref.shape[0]
    cnt = TOP_K * tm

    def step(s):
        @pl.when(i + 1 < nt)
        def _():
            _gather_tiles(posn_ref, ys_hbm, ybuf.at[1 - s], gsem.at[1 - s], 0, cnt)

        pltpu.make_async_copy(ys_hbm.at[pl.ds(0, cnt * SUB)], ybuf.at[s], gsem.at[s]).wait()
        gates = gt_ref[...]
        gk = [jnp.broadcast_to(gates[:, k:k + 1], (tm, LANE)) for k in range(TOP_K)]
        chunks = []
        for j in range(x_ref.shape[1] // LANE):
            parts = [gk[k] * ybuf[s, pl.ds(k * tm * SUB + j, tm, stride=SUB), :] for k in range(TOP_K)]
            chunks.append((parts[0] + parts[1]) + (parts[2] + parts[3]))
        o_ref[...] = _rms(x_ref[...] + jnp.concatenate(chunks, axis=1)) * g_ref[...]

    @pl.when(i == 0)
    def _():
        _gather_tiles(pos_ref, ys_hbm, ybuf.at[0], gsem.at[0], 0, cnt)

    for s in range(2):
        pl.when(lax.rem(i, 2) == s)(functools.partial(step, s))


def _final_call(x2, ys, pos_tiles, gates, g_final, tok0, ntok):
    d = x2.shape[1]
    assert TOP_K == 4 and d == SUB * LANE
    tm = FIN_TM
    assert tok0 % tm == 0 and ntok % tm == 0
    off, nt = tok0 // tm, ntok // tm
    smem = functools.partial(pl.BlockSpec, (1, 1, TOP_K * tm), memory_space=pltpu.SMEM)
    return pl.pallas_call(
        functools.partial(_final_kernel, nt=nt),
        grid=(nt,),
        in_specs=[
            smem(lambda i: (off + i, 0, 0)),
            smem(lambda i: (off + jnp.minimum(i + 1, nt - 1), 0, 0)),
            pl.BlockSpec((tm, d), lambda i: (off + i, 0)),
            pl.BlockSpec((tm, TOP_K), lambda i: (off + i, 0)),
            pl.BlockSpec((1, d), lambda i: (0, 0)),
            pl.BlockSpec(memory_space=pl.ANY),
        ],
        out_specs=pl.BlockSpec((tm, d), lambda i: (i, 0)),
        out_shape=jax.ShapeDtypeStruct((ntok, d), F32),
        scratch_shapes=[pltpu.VMEM((2, TOP_K * tm * SUB, LANE), F32), pltpu.SemaphoreType.DMA((2,))],
        compiler_params=_cparams(("arbitrary",)),
        name="final",
    )(pos_tiles, pos_tiles, x2, gates, g_final, ys)


def _prep_weights(w_in, w_uq, w_ukv, t_max):
    assert LANE - QK_HEAD == ROPE_DIM
    o = np.cumsum((0, Q_LORA, KV_LORA, ROPE_DIM, NA_WIDTH, NA_WIDTH, NA_WIDTH))
    w_cq, w_ckv, w_kpe = w_in[:, o[0]:o[1]], w_in[:, o[1]:o[2]], w_in[:, o[2]:o[3]]
    w_naq, w_nak, w_nav = w_in[:, o[3]:o[4]], w_in[:, o[4]:o[5]], w_in[:, o[5]:o[6]]
    half = ROPE_DIM // 2
    swap = np.concatenate([np.arange(half, ROPE_DIM), np.arange(half)])
    win_r = jnp.concatenate([w_naq, w_nak, w_nav, w_cq, w_ckv, jnp.zeros((w_in.shape[0], QK_NOPE), w_in.dtype),
                             w_kpe, w_kpe[:, swap]], axis=1).astype(BF16)

    uq = w_uq.reshape(Q_LORA, MLA_HEADS, QK_HEAD)
    wq = jnp.concatenate([uq, uq[:, :, QK_NOPE:][:, :, swap]], axis=2).reshape(Q_LORA, MLA_HEADS * LANE).astype(BF16)
    ukv = w_ukv.reshape(KV_LORA, MLA_HEADS, QK_NOPE + V_HEAD)
    wuk = jnp.pad(ukv[:, :, :QK_NOPE], ((0, 0), (0, 0), (0, LANE - QK_NOPE)))
    wuk = wuk.reshape(KV_LORA, MLA_HEADS * LANE).astype(BF16)
    wuv = ukv[:, :, QK_NOPE:].reshape(KV_LORA, MLA_WIDTH).astype(BF16)

    pos = jnp.arange(t_max, dtype=F32)
    inv = 1.0 / (ROPE_THETA ** (jnp.arange(0, ROPE_DIM, 2, dtype=F32) / ROPE_DIM))
    ang = pos[:, None] * inv[None, :]
    cos, sin = jnp.cos(ang), jnp.sin(ang)
    c32 = jnp.concatenate([cos, cos], axis=1)
    s32 = jnp.concatenate([-sin, sin], axis=1)
    z = lambda w: jnp.zeros((t_max, w), F32)
    cak = jnp.concatenate([z(QK_NOPE), c32, z(LANE - QK_HEAD)], axis=1)
    sbk = jnp.concatenate([z(QK_NOPE), s32, z(LANE - QK_HEAD)], axis=1)
    scale = QK_HEAD ** -0.5 * LOG2E
    caq = scale * jnp.concatenate([jnp.ones((t_max, QK_NOPE), F32), c32, z(LANE - QK_HEAD)], axis=1)
    tab = jnp.concatenate([caq, scale * sbk, cak, sbk], axis=1)
    return win_r, wq, wuk, wuv, tab


def _route(rec, cnt, bm, tm):
    n = rec.shape[1]
    a = n * TOP_K
    nblk = a // bm + N_EXPERTS
    expert = rec[0:TOP_K].astype(jnp.int32)
    rank = rec[TOP_K:2 * TOP_K].astype(jnp.int32)
    gates = rec[2 * TOP_K:3 * TOP_K].T
    counts = cnt[:, 0].astype(jnp.int32)
    start = jnp.cumsum(counts) - counts
    padded = ((counts + bm - 1) // bm) * bm
    pad_end = jnp.cumsum(padded)
    pad_start = pad_end - padded
    blk0 = jnp.arange(nblk, dtype=jnp.int32) * bm
    blk_e = jnp.minimum(jnp.sum((pad_end[None, :] <= blk0[:, None]).astype(jnp.int32), axis=1), N_EXPERTS - 1)
    nused = (pad_end[-1:] // bm).astype(jnp.int32)
    experts = jnp.arange(N_EXPERTS, dtype=jnp.int32)[:, None, None]
    seg = jnp.sum(jnp.where(expert[None] == experts, pad_start[:, None, None], 0), axis=0)
    pos = seg + rank
    pos_tiles = (pos * SUB).reshape(TOP_K, n // tm, tm).transpose(1, 0, 2).reshape(n // tm, 1, TOP_K * tm)
    tok = jnp.broadcast_to(jnp.arange(n, dtype=jnp.int32)[None, :], (TOP_K, n))
    _, tok_sorted = lax.sort((pos.reshape(-1), tok.reshape(-1)), num_keys=1)
    slot_rank = (blk0 - pad_start[blk_e])[:, None] + jnp.arange(bm, dtype=jnp.int32)[None, :]
    valid = slot_rank < counts[blk_e][:, None]
    src = jnp.where(valid, tok_sorted[jnp.clip(start[blk_e][:, None] + slot_rank, 0, a - 1)], 0)
    return (src * SUB)[:, None, :], blk_e.astype(jnp.int32), nused, pos_tiles, gates


def kernel(x_prompt, x_sample, g_attn_norm, w_in, g_q_lora, w_uq, g_kv_lora, w_ukv, na_rpb, g_out_mla,
           g_out_na, w_out, g_ffn_norm, w_router, b_router, w_gate, b_gate, w_up, b_up, w_down, b_down,
           g_final):
    assert g_attn_norm.shape[0] == 1, "single-layer trunk"
    d = x_prompt.shape[-1]
    groups = [x_prompt, x_sample]
    shapes = [(g.shape[0], g.shape[1]) for g in groups]
    for _, t in shapes:
        assert t % TOK_TILE == 0 and t % GRID_W == 0 and (t // GRID_W) % NA_RQ == 0
        assert t // GRID_W >= NA_WROWS + NA_RQ
    xa, xb = (g.reshape(-1, d) for g in groups)
    n = xa.shape[0] + xb.shape[0]
    t_max = max(t for _, t in shapes)

    win_r, wq, wuk, wuv, tab = _prep_weights(w_in[0], w_uq[0], w_ukv[0], t_max)
    pos_blocks = [(b * t // TOK_TILE, t // TOK_TILE) for b, t in shapes]
    row = lambda v: v.reshape(1, -1)
    q, k, v, naq, nak, nav = _proj_call(xa, xb, row(g_attn_norm[0]), win_r, row(g_q_lora[0]), wq,
                                        row(g_kv_lora[0]), wuk, wuv, tab, pos_blocks)

    bias = _na_bias_table(na_rpb[0])
    a_all = jnp.zeros((n, MLA_WIDTH), BF16)
    n_all = jnp.zeros((n, NA_WIDTH), BF16)
    tok0 = 0
    for b, t in shapes:
        assert tok0 % t == 0 and n % t == 0
        boff = tok0 // t
        view = lambda arr: arr.reshape(n // t, t, arr.shape[-1])
        a_all = _mla_call(view(q), view(k), view(v), b, t, boff, view(a_all)).reshape(n, MLA_WIDTH)
        n_all = _na_call(view(naq), view(nak), view(nav), bias, b, t, boff, view(n_all)).reshape(n, NA_WIDTH)
        tok0 += b * t

    wr = w_router[0].T
    wr_hi = wr.astype(BF16)
    wr_lo = (wr - wr_hi.astype(F32)).astype(BF16)
    b_r = b_router[0].reshape(N_EXPERTS, 1)
    x2, hf, rec, cnt = _mix_call(a_all, n_all, xa, xb, row(g_out_mla[0]), row(g_out_na[0]), w_out[0].astype(BF16),
                                 row(g_ffn_norm[0]), wr_hi, wr_lo, b_r)

    src, blk_e, nused, pos_tiles, gates = _route(rec, cnt, MOE_BM, FIN_TM)
    ys = _moe_call(blk_e, nused, src, w_gate[0], b_gate[0][:, None, :], w_up[0], b_up[0][:, None, :],
                   w_down[0], b_down[0][:, None, :], hf)

    outs, tok0 = [], 0
    for b, t in shapes:
        outs.append(_final_call(x2, ys, pos_tiles, gates, row(g_final), tok0, b * t).reshape(b, t, d))
        tok0 += b * t
    return tuple(outs)
```
